```python
import math
import jax, jax.numpy as jnp
from jax import lax
import numpy as np

D_MODEL = 1024
BATCH = 4
SEQ = 8192
DEPTH = 1

PLE_DIM = 256
MIX_WIDTH = D_MODEL
ATTN_WIDTH = MIX_WIDTH // 2
SGU_WIDTH = MIX_WIDTH - ATTN_WIDTH
HEAD_DIM = 64
N_ATTN_HEADS = ATTN_WIDTH // HEAD_DIM
N_SGU_GROUPS = 4
SGU_GROUP_DIM = SGU_WIDTH // N_SGU_GROUPS
SGU_CHUNK = 128
DILATION_PAIRS = ((128, 1), (512, 4), (2048, 16))
QBLK = 128
D_FF = ((8 * D_MODEL + 3 * 256 - 1) // (3 * 256)) * 256
PROJ_COLS = 3 * ATTN_WIDTH + 2 * SGU_WIDTH
EPS = 1e-6
NEG = -1e30

kernel_name = "hybrid_dilated_attn_gmlp_block"


def rmsnorm(x, g):
    xf = x.astype(jnp.float32)
    y = xf * lax.rsqrt(jnp.mean(xf * xf, axis=-1, keepdims=True) + EPS)
    return (y * g.astype(jnp.float32)).astype(x.dtype)


def layernorm(x, g, b):
    xf = x.astype(jnp.float32)
    mu = jnp.mean(xf, axis=-1, keepdims=True)
    xc = xf - mu
    y = xc * lax.rsqrt(jnp.mean(xc * xc, axis=-1, keepdims=True) + EPS)
    return (y * g.astype(jnp.float32) + b.astype(jnp.float32)).astype(x.dtype)


def dilated_branch(q, k, v, slopes, window, dilation):
    B, H, S, hd = q.shape
    span = dilation * QBLK
    s_pad = -(-S // span) * span
    M = s_pad // dilation
    nb = M // QBLK
    n_steps = window // dilation
    pad = ((0, 0), (0, 0), (0, s_pad - S), (0, 0))

    def to_blocks(t):
        t = jnp.pad(t, pad).reshape(B, H, M, dilation, hd).transpose(0, 1, 3, 2, 4)
        return t.reshape(B, H, dilation, nb, QBLK, hd)

    def with_prev(t):
        prev = jnp.pad(t, ((0, 0), (0, 0), (0, 0), (1, 0), (0, 0), (0, 0)))[:, :, :, :-1]
        return jnp.concatenate([prev, t], axis=-2)

    qb = to_blocks(q)
    kc = with_prev(to_blocks(k))
    vc = with_prev(to_blocks(v))
    s = jnp.einsum('bhrnqc,bhrnkc->bhrnqk', qb, kc)

    qi = jnp.arange(QBLK)[:, None]
    ki = jnp.arange(2 * QBLK)[None, :]
    steps = QBLK + qi - ki
    blk = jnp.arange(nb)[:, None, None]
    valid = (steps >= 0) & (steps <= n_steps) & (blk * QBLK - QBLK + ki >= 0)
    dist = (jnp.clip(steps, 0, None) * dilation).astype(jnp.float32)
    bias = -slopes[:, None, None] * dist[None]
    s = s + bias[None, :, None, None]
    s = jnp.where(valid[None, None, None], s, NEG)
    mx = jnp.max(s, axis=-1, keepdims=True)
    e = jnp.exp(s - mx)
    den = jnp.sum(e, axis=-1)
    o = jnp.einsum('bhrnqk,bhrnkc->bhrnqc', e, vc) / den[..., None]
    lse = mx[..., 0] + jnp.log(den)
    o = o.reshape(B, H, dilation, M, hd).transpose(0, 1, 3, 2, 4).reshape(B, H, s_pad, hd)[:, :, :S]
    lse = lse.reshape(B, H, dilation, M).transpose(0, 1, 3, 2).reshape(B, H, s_pad)[:, :, :S]
    return o, lse


def dilated_attention(q, k, v):
    B, S, _ = q.shape
    dtype = q.dtype

    def heads(t):
        return t.reshape(B, S, N_ATTN_HEADS, HEAD_DIM).transpose(0, 2, 1, 3).astype(jnp.float32)

    qh = heads(q) * (HEAD_DIM ** -0.5)
    kh, vh = heads(k), heads(v)
    slopes = 2.0 ** (-8.0 * (jnp.arange(N_ATTN_HEADS, dtype=jnp.float32) + 1.0) / N_ATTN_HEADS)
    outs, lses = [], []
    for window, dilation in DILATION_PAIRS:
        o, l = dilated_branch(qh, kh, vh, slopes, window, dilation)
        outs.append(o)
        lses.append(l)
    w = jax.nn.softmax(jnp.stack(lses, axis=0), axis=0)
    out = jnp.sum(w[..., None] * jnp.stack(outs, axis=0), axis=0)
    return out.transpose(0, 2, 1, 3).reshape(B, S, ATTN_WIDTH).astype(dtype)


def spatial_gating(u, z, ln_g, ln_b, w_s, b_s):
    B, S, _ = u.shape
    nc = S // SGU_CHUNK
    u = jax.nn.gelu(u).reshape(B, S, N_SGU_GROUPS, SGU_GROUP_DIM)
    z = jax.nn.gelu(z).reshape(B, S, N_SGU_GROUPS, SGU_GROUP_DIM)
    z = layernorm(z, ln_g, ln_b)
    zc = z.reshape(B, nc, SGU_CHUNK, N_SGU_GROUPS, SGU_GROUP_DIM)
    causal = jnp.tril(jnp.ones((SGU_CHUNK, SGU_CHUNK), dtype=w_s.dtype))
    wm = w_s * causal[None]
    mixed = jnp.einsum('gij,bnjgc->bnigc', wm, zc) + b_s.T[None, None, :, :, None]
    out = u * mixed.reshape(B, S, N_SGU_GROUPS, SGU_GROUP_DIM)
    return out.reshape(B, S, SGU_WIDTH)


def setup_inputs(seed: int = 0) -> dict:
    key = jax.random.key(seed)
    ks = jax.random.split(key, 20)
    f32 = jnp.float32

    def nrm(k, shape, scale):
        return jax.random.normal(k, shape, f32) * scale

    def gain(k, shape):
        return 1.0 + 0.05 * jax.random.normal(k, shape, f32)

    L = DEPTH
    return {
        "x": jax.random.normal(ks[0], (BATCH, SEQ, D_MODEL), f32),
        "p": jax.random.normal(ks[1], (DEPTH, BATCH, SEQ, PLE_DIM), f32),
        "ln_pre_mix": gain(ks[2], (L, D_MODEL)),
        "w_in": nrm(ks[3], (L, D_MODEL, PROJ_COLS), D_MODEL ** -0.5),
        "sgu_ln_g": gain(ks[4], (L, SGU_GROUP_DIM)),
        "sgu_ln_b": nrm(ks[5], (L, SGU_GROUP_DIM), 0.02),
        "w_spatial": nrm(ks[6], (L, N_SGU_GROUPS, SGU_CHUNK, SGU_CHUNK), SGU_CHUNK ** -0.5),
        "b_spatial": gain(ks[7], (L, N_SGU_GROUPS, SGU_CHUNK)),
        "attn_out_norm": gain(ks[8], (L, ATTN_WIDTH)),
        "sgu_out_norm": gain(ks[9], (L, SGU_WIDTH)),
        "w_out": nrm(ks[10], (L, MIX_WIDTH, D_MODEL), MIX_WIDTH ** -0.5),
        "ln_post_mix": gain(ks[11], (L, D_MODEL)),
        "ln_pre_ffn": gain(ks[12], (L, D_MODEL)),
        "w_gate_up": nrm(ks[13], (L, D_MODEL, 2 * D_FF), D_MODEL ** -0.5),
        "w_down": nrm(ks[14], (L, D_FF, D_MODEL), D_FF ** -0.5),
        "ln_post_ffn": gain(ks[15], (L, D_MODEL)),
        "w_pe_gate": nrm(ks[16], (L, D_MODEL, D_MODEL), D_MODEL ** -0.5),
        "b_pe_gate": nrm(ks[17], (L, D_MODEL), 0.02),
        "w_pe_proj": nrm(ks[18], (L, PLE_DIM, D_MODEL), PLE_DIM ** -0.5),
    }


def reference(x, p, ln_pre_mix, w_in, sgu_ln_g, sgu_ln_b, w_spatial, b_spatial,
              attn_out_norm, sgu_out_norm, w_out, ln_post_mix, ln_pre_ffn, w_gate_up,
              w_down, ln_post_ffn, w_pe_gate, b_pe_gate, w_pe_proj):
    h = x
    splits = [ATTN_WIDTH, 2 * ATTN_WIDTH, 3 * ATTN_WIDTH, 3 * ATTN_WIDTH + SGU_WIDTH]
    for i in range(DEPTH):
        a = rmsnorm(h, ln_pre_mix[i])
        proj = a @ w_in[i]
        q, k, v, u, z = jnp.split(proj, splits, axis=-1)
        attn = dilated_attention(q, k, v)
        sgu = spatial_gating(u, z, sgu_ln_g[i], sgu_ln_b[i], w_spatial[i], b_spatial[i])
        groups = jnp.concatenate([rmsnorm(attn, attn_out_norm[i]),
                                  rmsnorm(sgu, sgu_out_norm[i])], axis=-1)
        mixed = groups @ w_out[i]
        h = h + rmsnorm(mixed, ln_post_mix[i])
        f = rmsnorm(h, ln_pre_ffn[i])
        g, up = jnp.split(f @ w_gate_up[i], 2, axis=-1)
        y = (jax.nn.silu(g) * up) @ w_down[i]
        h = h + rmsnorm(y, ln_post_ffn[i])
        gate = jax.nn.sigmoid(h @ w_pe_gate[i] + b_pe_gate[i])
        h = h + gate * (p[i] @ w_pe_proj[i])
    return h
```

```python
import functools

import jax
import jax.numpy as jnp
from jax import lax
from jax.experimental import pallas as pl
from jax.experimental.pallas import tpu as pltpu

F32 = jnp.float32
BF16 = jnp.bfloat16

HEAD_DIM = 64
QBLK = 128
WIN = 2 * QBLK
DILATIONS = (1, 4, 16)
SPAN = QBLK * DILATIONS[-1]
SGU_CHUNK = 128
N_SGU_GROUPS = 4
EPS = 1e-6
NEG = -1e30

LANES = 128
HEADS_PER_SLAB = LANES // HEAD_DIM

PROJ_TILE = 512
TAIL_TILE = 512
VMEM_LIMIT = 56 * 1024 * 1024


def _rms(x, g):
    return x * lax.rsqrt(jnp.mean(x * x, axis=-1, keepdims=True) + EPS) * g


def _const_spec(shape):
    zeros = (0,) * len(shape)
    return pl.BlockSpec(shape, lambda *_: zeros, pipeline_mode=pl.Buffered(1))


def _proj_sgu_kernel(x_ref, g_ref, w_ref, lng_ref, lnb_ref, ws_ref, bs_ref, gout_ref,
                     q_ref, k_ref, v_ref, sgu_ref, *, attn_w, sgu_w):
    tm = x_ref.shape[0]
    a = _rms(x_ref[...], g_ref[...]).astype(BF16)
    proj = jnp.dot(a, w_ref[...], preferred_element_type=F32)

    n_slabs = attn_w // LANES
    for s in range(n_slabs):
        q_ref[s] = proj[:, s * LANES:(s + 1) * LANES] * (HEAD_DIM ** -0.5)
        k_ref[s] = proj[:, attn_w + s * LANES: attn_w + (s + 1) * LANES]
        v_ref[s] = proj[:, 2 * attn_w + s * LANES: 2 * attn_w + (s + 1) * LANES]

    gd = sgu_w // N_SGU_GROUPS
    row = lax.broadcasted_iota(jnp.int32, (SGU_CHUNK, SGU_CHUNK), 0)
    col = lax.broadcasted_iota(jnp.int32, (SGU_CHUNK, SGU_CHUNK), 1)
    causal = row >= col
    outs = []
    ssq = jnp.zeros((tm, 1), F32)
    for g in range(N_SGU_GROUPS):
        u = jax.nn.gelu(proj[:, 3 * attn_w + g * gd: 3 * attn_w + (g + 1) * gd])
        z = jax.nn.gelu(proj[:, 3 * attn_w + sgu_w + g * gd: 3 * attn_w + sgu_w + (g + 1) * gd])
        zc = z - jnp.mean(z, axis=-1, keepdims=True)
        zn = zc * lax.rsqrt(jnp.mean(zc * zc, axis=-1, keepdims=True) + EPS)
        zn = (zn * lng_ref[...] + lnb_ref[...]).astype(BF16)
        wm = jnp.where(causal, ws_ref[g], 0.0).astype(BF16)
        mixed = [jnp.dot(wm, zn[c * SGU_CHUNK:(c + 1) * SGU_CHUNK], preferred_element_type=F32)
                 + bs_ref[g] for c in range(tm // SGU_CHUNK)]
        o = u * jnp.concatenate(mixed, axis=0)
        ssq = ssq + jnp.sum(o * o, axis=-1, keepdims=True)
        outs.append(o)
    scale = lax.rsqrt(ssq / sgu_w + EPS)
    for g in range(N_SGU_GROUPS):
        sgu_ref[:, g * gd:(g + 1) * gd] = (outs[g] * scale * gout_ref[:, g * gd:(g + 1) * gd]).astype(BF16)


def _proj_sgu(x2, g_pre, w_in, ln_g, ln_b, w_sp, b_sp, g_sgu, attn_w, sgu_w):
    n, d = x2.shape
    tm = PROJ_TILE
    n_slabs = attn_w // LANES
    slab = jax.ShapeDtypeStruct((n_slabs, n, LANES), F32)
    slab_spec = pl.BlockSpec((n_slabs, tm, LANES), lambda i: (0, i, 0))
    return pl.pallas_call(
        functools.partial(_proj_sgu_kernel, attn_w=attn_w, sgu_w=sgu_w),
        grid=(n // tm,),
        in_specs=[
            pl.BlockSpec((tm, d), lambda i: (i, 0)),
            _const_spec(g_pre.shape), _const_spec(w_in.shape), _const_spec(ln_g.shape),
            _const_spec(ln_b.shape), _const_spec(w_sp.shape), _const_spec(b_sp.shape),
            _const_spec(g_sgu.shape),
        ],
        out_specs=[slab_spec, slab_spec, slab_spec, pl.BlockSpec((tm, sgu_w), lambda i: (i, 0))],
        out_shape=[slab, slab, slab, jax.ShapeDtypeStruct((n, sgu_w), BF16)],
        compiler_params=pltpu.CompilerParams(dimension_semantics=("arbitrary",),
                                             vmem_limit_bytes=VMEM_LIMIT),
        name="proj_sgu",
    )(x2, g_pre, w_in, ln_g, ln_b, w_sp, b_sp, g_sgu)


K1_ROWS = QBLK + SPAN
K4_RES_ROWS = QBLK + SPAN // 4
K16_RES_ROWS = WIN


def _attn_kernel(slopes_ref, q_ref, k_ref, v_ref, o_ref,
                 bias_ref, q1, q4, q16, k1, v1, k4, v4, k16, v16, aq, ak, av, ob, lb):
    slab = pl.program_id(1)
    span = pl.program_id(2)
    lane = lax.broadcasted_iota(jnp.int32, (QBLK, LANES), 1)
    head0 = lane < HEAD_DIM

    @pl.when(span == 0)
    def _start_of_sequence():
        zeros = jnp.zeros((QBLK, LANES), BF16)
        for ref in (k1, v1):
            ref[0:QBLK] = zeros
        for ref in (k4, v4):
            for r in range(4):
                ref[r * K4_RES_ROWS: r * K4_RES_ROWS + QBLK] = zeros
        for ref in (k16, v16):
            for r in range(16):
                ref[r * K16_RES_ROWS: r * K16_RES_ROWS + QBLK] = zeros
        i = lax.broadcasted_iota(jnp.int32, (WIN, WIN), 0)
        j = lax.broadcasted_iota(jnp.int32, (WIN, WIN), 1)
        steps = QBLK + (i & (QBLK - 1)) - j
        valid = (steps >= 0) & (steps <= QBLK)
        slope = jnp.where(i < QBLK, slopes_ref[HEADS_PER_SLAB * slab],
                          slopes_ref[HEADS_PER_SLAB * slab + 1])
        for b, d in enumerate(DILATIONS):
            bias = jnp.where(valid, -slope * (steps * d).astype(F32), NEG)
            bias_ref[b, 1] = bias
            bias_ref[b, 0] = jnp.where(j >= QBLK, bias, NEG)

    def put_q(dst, base, blk):
        dst[base: base + QBLK] = jnp.where(head0, blk, 0.0).astype(BF16)
        dst[base + QBLK: base + WIN] = jnp.where(head0, 0.0, blk).astype(BF16)

    k1[QBLK:K1_ROWS] = k_ref[0].astype(BF16)
    v1[QBLK:K1_ROWS] = v_ref[0].astype(BF16)
    for nb in range(SPAN // QBLK):
        put_q(q1, nb * WIN, q_ref[0, nb * QBLK:(nb + 1) * QBLK, :])
    for r in range(4):
        for src, tmp in ((q_ref, aq), (k_ref, ak), (v_ref, av)):
            tmp[r] = src[0, pl.ds(r, SPAN // 4, stride=4), :]
        k4[r * K4_RES_ROWS + QBLK:(r + 1) * K4_RES_ROWS] = ak[r].astype(BF16)
        v4[r * K4_RES_ROWS + QBLK:(r + 1) * K4_RES_ROWS] = av[r].astype(BF16)
        for nb in range(4):
            put_q(q4, (r * 4 + nb) * WIN, aq[r, nb * QBLK:(nb + 1) * QBLK, :])
        for r2 in range(4):
            r16 = 4 * r2 + r
            k16[r16 * K16_RES_ROWS + QBLK:(r16 + 1) * K16_RES_ROWS] = (
                ak[r, pl.ds(r2, QBLK, stride=4), :].astype(BF16))
            v16[r16 * K16_RES_ROWS + QBLK:(r16 + 1) * K16_RES_ROWS] = (
                av[r, pl.ds(r2, QBLK, stride=4), :].astype(BF16))
            put_q(q16, r16 * WIN, aq[r, pl.ds(r2, QBLK, stride=4), :])

    not_first = jnp.where(span == 0, 0, 1)

    def block(b, qs, ks, vs, qoff, koff, sel):
        q = qs[pl.ds(pl.multiple_of(qoff, QBLK), WIN), :]
        kw = ks[pl.ds(pl.multiple_of(koff, QBLK), WIN), :]
        vw = vs[pl.ds(pl.multiple_of(koff, QBLK), WIN), :]
        s = lax.dot_general(q, kw, (((1,), (1,)), ((), ())), preferred_element_type=F32)
        s = s + bias_ref[b, sel]
        m = jnp.max(s, axis=1, keepdims=True)
        p = jnp.exp(s - m)
        l = jnp.sum(p, axis=1, keepdims=True)
        acc = jnp.dot(p.astype(BF16), vw, preferred_element_type=F32)
        o = acc / l
        lse = m + jnp.log(l)
        o_t = jnp.where(head0, o[0:QBLK], o[QBLK:WIN])
        lse_t = jnp.where(head0, lse[0:QBLK], lse[QBLK:WIN])
        return o_t, lse_t

    def branch1(nb, c):
        sel = jnp.where(nb == 0, not_first, 1)
        o_t, lse_t = block(0, q1, k1, v1, nb * WIN, nb * QBLK, sel)
        start = pl.multiple_of(nb * QBLK, QBLK)
        ob[0, pl.ds(start, QBLK), :] = o_t
        lb[0, pl.ds(start, QBLK), :] = lse_t
        return c

    def branch4(it, c):
        r = it // 4
        nb = it % 4
        sel = jnp.where(nb == 0, not_first, 1)
        o_t, lse_t = block(1, q4, k4, v4, it * WIN, r * K4_RES_ROWS + nb * QBLK, sel)
        start = nb * (4 * QBLK) + r
        ob[1, pl.ds(start, QBLK, stride=4), :] = o_t
        lb[1, pl.ds(start, QBLK, stride=4), :] = lse_t
        return c

    def branch16(r, c):
        o_t, lse_t = block(2, q16, k16, v16, r * WIN, r * K16_RES_ROWS, not_first)
        ob[2, pl.ds(r, QBLK, stride=16), :] = o_t
        lb[2, pl.ds(r, QBLK, stride=16), :] = lse_t
        return c

    lax.fori_loop(0, SPAN // QBLK, branch1, 0)
    lax.fori_loop(0, 16, branch4, 0)
    lax.fori_loop(0, 16, branch16, 0)

    def merge(c, carry):
        rows = pl.ds(pl.multiple_of(c * WIN, WIN), WIN)
        l0, l1, l2 = lb[0, rows, :], lb[1, rows, :], lb[2, rows, :]
        mx = jnp.maximum(jnp.maximum(l0, l1), l2)
        w0, w1, w2 = jnp.exp(l0 - mx), jnp.exp(l1 - mx), jnp.exp(l2 - mx)
        num = w0 * ob[0, rows, :] + w1 * ob[1, rows, :] + w2 * ob[2, rows, :]
        o_ref[rows, :] = (num / (w0 + w1 + w2)).astype(o_ref.dtype)
        return carry

    lax.fori_loop(0, SPAN // WIN, merge, 0)

    for ref in (k1, v1):
        ref[0:QBLK] = ref[SPAN:K1_ROWS]
    for ref in (k4, v4):
        for r in range(4):
            ref[r * K4_RES_ROWS: r * K4_RES_ROWS + QBLK] = ref[(r + 1) * K4_RES_ROWS - QBLK:(r + 1) * K4_RES_ROWS]
    for ref in (k16, v16):
        for r in range(16):
            ref[r * K16_RES_ROWS: r * K16_RES_ROWS + QBLK] = ref[r * K16_RES_ROWS + QBLK:(r + 1) * K16_RES_ROWS]


def _attention(q, k, v, slopes, batch, seq):
    n_slabs, n, _ = q.shape
    assert seq % SPAN == 0
    spans = seq // SPAN
    in_spec = pl.BlockSpec((1, SPAN, LANES), lambda b, s, t: (s, b * spans + t, 0))
    return pl.pallas_call(
        _attn_kernel,
        grid=(batch, n_slabs, spans),
        in_specs=[pl.BlockSpec(memory_space=pltpu.SMEM), in_spec, in_spec, in_spec],
        out_specs=pl.BlockSpec((SPAN, LANES), lambda b, s, t: (b * spans + t, s)),
        out_shape=jax.ShapeDtypeStruct((n, n_slabs * LANES), BF16),
        scratch_shapes=[
            pltpu.VMEM((len(DILATIONS), 2, WIN, WIN), F32),
            pltpu.VMEM((2 * SPAN, LANES), BF16),
            pltpu.VMEM((2 * SPAN, LANES), BF16),
            pltpu.VMEM((2 * SPAN, LANES), BF16),
            pltpu.VMEM((K1_ROWS, LANES), BF16),
            pltpu.VMEM((K1_ROWS, LANES), BF16),
            pltpu.VMEM((4 * K4_RES_ROWS, LANES), BF16),
            pltpu.VMEM((4 * K4_RES_ROWS, LANES), BF16),
            pltpu.VMEM((16 * K16_RES_ROWS, LANES), BF16),
            pltpu.VMEM((16 * K16_RES_ROWS, LANES), BF16),
            pltpu.VMEM((4, SPAN // 4, LANES), F32),
            pltpu.VMEM((4, SPAN // 4, LANES), F32),
            pltpu.VMEM((4, SPAN // 4, LANES), F32),
            pltpu.VMEM((len(DILATIONS), SPAN, LANES), F32),
            pltpu.VMEM((len(DILATIONS), SPAN, LANES), F32),
        ],
        compiler_params=pltpu.CompilerParams(
            dimension_semantics=("arbitrary", "arbitrary", "arbitrary"),
            vmem_limit_bytes=VMEM_LIMIT),
        name="attn",
    )(slopes, q, k, v)


def _tail_kernel(x_ref, attn_ref, sgu_ref, p_ref, ga_ref, wout_ref, gpm_ref, gpf_ref,
                 wgu_ref, wd_ref, gpo_ref, wpg_ref, bpg_ref, wpp_ref, o_ref):
    d_ff = wd_ref.shape[0]
    an = _rms(attn_ref[...].astype(F32), ga_ref[...]).astype(BF16)
    groups = jnp.concatenate([an, sgu_ref[...]], axis=-1)
    mixed = jnp.dot(groups, wout_ref[...], preferred_element_type=F32)
    h = x_ref[...] + _rms(mixed, gpm_ref[...])
    f = _rms(h, gpf_ref[...]).astype(BF16)
    gu = jnp.dot(f, wgu_ref[...], preferred_element_type=F32)
    hid = (jax.nn.silu(gu[:, :d_ff]) * gu[:, d_ff:]).astype(BF16)
    y = jnp.dot(hid, wd_ref[...], preferred_element_type=F32)
    h = h + _rms(y, gpo_ref[...])
    gate = jax.nn.sigmoid(jnp.dot(h.astype(BF16), wpg_ref[...], preferred_element_type=F32)
                          + bpg_ref[...])
    pe = jnp.dot(p_ref[...].astype(BF16), wpp_ref[...], preferred_element_type=F32)
    o_ref[...] = h + gate * pe


def _tail(x2, attn, sgu, p2, *params):
    n, d = x2.shape
    tm = TAIL_TILE

    def tile(a):
        return pl.BlockSpec((tm, a.shape[1]), lambda i: (i, 0))

    return pl.pallas_call(
        _tail_kernel,
        grid=(n // tm,),
        in_specs=[tile(x2), tile(attn), tile(sgu), tile(p2)] + [_const_spec(a.shape) for a in params],
        out_specs=pl.BlockSpec((tm, d), lambda i: (i, 0)),
        out_shape=jax.ShapeDtypeStruct((n, d), F32),
        compiler_params=pltpu.CompilerParams(dimension_semantics=("arbitrary",),
                                             vmem_limit_bytes=VMEM_LIMIT),
        name="tail",
    )(x2, attn, sgu, p2, *params)


def kernel(x, p, ln_pre_mix, w_in, sgu_ln_g, sgu_ln_b, w_spatial, b_spatial, attn_out_norm,
           sgu_out_norm, w_out, ln_post_mix, ln_pre_ffn, w_gate_up, w_down, ln_post_ffn,
           w_pe_gate, b_pe_gate, w_pe_proj):
    batch, seq, d = x.shape
    depth = w_in.shape[0]
    attn_w = attn_out_norm.shape[1]
    sgu_w = sgu_out_norm.shape[1]
    n_heads = attn_w // HEAD_DIM
    slopes = 2.0 ** (-8.0 * (jnp.arange(n_heads, dtype=F32) + 1.0) / n_heads)

    def row(a):
        return a.reshape(1, -1)

    h = x.reshape(batch * seq, d)
    for i in range(depth):
        q, k, v, sgu = _proj_sgu(
            h, row(ln_pre_mix[i]), w_in[i].astype(BF16), row(sgu_ln_g[i]), row(sgu_ln_b[i]),
            w_spatial[i], b_spatial[i][:, :, None], row(sgu_out_norm[i]), attn_w, sgu_w)
        attn = _attention(q, k, v, slopes, batch, seq)
        h = _tail(h, attn, sgu, p[i].reshape(batch * seq, -1),
                  row(attn_out_norm[i]), w_out[i].astype(BF16), row(ln_post_mix[i]),
                  row(ln_pre_ffn[i]), w_gate_up[i].astype(BF16), w_down[i].astype(BF16),
                  row(ln_post_ffn[i]), w_pe_gate[i].astype(BF16), row(b_pe_gate[i]),
                  w_pe_proj[i].astype(BF16))
    return h.reshape(batch, seq, d)
```

```python
import functools

import jax
import jax.numpy as jnp
from jax import lax
from jax.experimental import pallas as pl
from jax.experimental.pallas import tpu as pltpu

F32 = jnp.float32
BF16 = jnp.bfloat16

HEAD_DIM = 64
QBLK = 128
WIN = 2 * QBLK
DILATIONS = (1, 4, 16)
SPAN = QBLK * DILATIONS[-1]
SGU_CHUNK = 128
N_SGU_GROUPS = 4
EPS = 1e-6
NEG = -1e30

LANES = 128
HEADS_PER_SLAB = LANES // HEAD_DIM
BLOCK_UNROLL = 16

PROJ_TILE = 512
TAIL_TILE = 512
VMEM_LIMIT = 56 * 1024 * 1024


def _rms(x, g):
    return x * lax.rsqrt(jnp.mean(x * x, axis=-1, keepdims=True) + EPS) * g


def _const_spec(shape):
    zeros = (0,) * len(shape)
    return pl.BlockSpec(shape, lambda *_: zeros, pipeline_mode=pl.Buffered(1))


def _proj_sgu_kernel(x_ref, g_ref, w_ref, lng_ref, lnb_ref, ws_ref, bs_ref, gout_ref,
                     q_ref, k_ref, v_ref, sgu_ref, *, attn_w, sgu_w):
    tm = x_ref.shape[0]
    a = _rms(x_ref[...], g_ref[...]).astype(BF16)
    proj = jnp.dot(a, w_ref[...], preferred_element_type=F32)

    n_slabs = attn_w // LANES
    for s in range(n_slabs):
        q_ref[s] = proj[:, s * LANES:(s + 1) * LANES] * (HEAD_DIM ** -0.5)
        k_ref[s] = proj[:, attn_w + s * LANES: attn_w + (s + 1) * LANES]
        v_ref[s] = proj[:, 2 * attn_w + s * LANES: 2 * attn_w + (s + 1) * LANES]

    gd = sgu_w // N_SGU_GROUPS
    row = lax.broadcasted_iota(jnp.int32, (SGU_CHUNK, SGU_CHUNK), 0)
    col = lax.broadcasted_iota(jnp.int32, (SGU_CHUNK, SGU_CHUNK), 1)
    causal = row >= col
    outs = []
    ssq = jnp.zeros((tm, 1), F32)
    for g in range(N_SGU_GROUPS):
        u = jax.nn.gelu(proj[:, 3 * attn_w + g * gd: 3 * attn_w + (g + 1) * gd])
        z = jax.nn.gelu(proj[:, 3 * attn_w + sgu_w + g * gd: 3 * attn_w + sgu_w + (g + 1) * gd])
        zc = z - jnp.mean(z, axis=-1, keepdims=True)
        zn = zc * lax.rsqrt(jnp.mean(zc * zc, axis=-1, keepdims=True) + EPS)
        zn = (zn * lng_ref[...] + lnb_ref[...]).astype(BF16)
        wm = jnp.where(causal, ws_ref[g], 0.0).astype(BF16)
        mixed = [jnp.dot(wm, zn[c * SGU_CHUNK:(c + 1) * SGU_CHUNK], preferred_element_type=F32)
                 + bs_ref[g] for c in range(tm // SGU_CHUNK)]
        o = u * jnp.concatenate(mixed, axis=0)
        ssq = ssq + jnp.sum(o * o, axis=-1, keepdims=True)
        outs.append(o)
    scale = lax.rsqrt(ssq / sgu_w + EPS)
    for g in range(N_SGU_GROUPS):
        sgu_ref[:, g * gd:(g + 1) * gd] = (outs[g] * scale * gout_ref[:, g * gd:(g + 1) * gd]).astype(BF16)


def _proj_sgu(x2, g_pre, w_in, ln_g, ln_b, w_sp, b_sp, g_sgu, attn_w, sgu_w):
    n, d = x2.shape
    tm = PROJ_TILE
    n_slabs = attn_w // LANES
    slab = jax.ShapeDtypeStruct((n_slabs, n, LANES), F32)
    slab_spec = pl.BlockSpec((n_slabs, tm, LANES), lambda i: (0, i, 0))
    return pl.pallas_call(
        functools.partial(_proj_sgu_kernel, attn_w=attn_w, sgu_w=sgu_w),
        grid=(n // tm,),
        in_specs=[
            pl.BlockSpec((tm, d), lambda i: (i, 0)),
            _const_spec(g_pre.shape), _const_spec(w_in.shape), _const_spec(ln_g.shape),
            _const_spec(ln_b.shape), _const_spec(w_sp.shape), _const_spec(b_sp.shape),
            _const_spec(g_sgu.shape),
        ],
        out_specs=[slab_spec, slab_spec, slab_spec, pl.BlockSpec((tm, sgu_w), lambda i: (i, 0))],
        out_shape=[slab, slab, slab, jax.ShapeDtypeStruct((n, sgu_w), BF16)],
        compiler_params=pltpu.CompilerParams(dimension_semantics=("arbitrary",),
                                             vmem_limit_bytes=VMEM_LIMIT),
        name="proj_sgu",
    )(x2, g_pre, w_in, ln_g, ln_b, w_sp, b_sp, g_sgu)


K1_ROWS = QBLK + SPAN
K4_RES_ROWS = QBLK + SPAN // 4
K16_RES_ROWS = WIN


def _attn_kernel(slopes_ref, q_ref, k_ref, v_ref, o_ref,
                 bias_ref, q4, q16, k1, v1, k4, v4, k16, v16, aq, ak, av, ob, lb):
    slab = pl.program_id(1)
    span = pl.program_id(2)
    lane = lax.broadcasted_iota(jnp.int32, (QBLK, LANES), 1)
    head0 = lane < HEAD_DIM

    @pl.when(span == 0)
    def _start_of_sequence():
        zeros = jnp.zeros((QBLK, LANES), BF16)
        k1[0:QBLK] = zeros
        v1[0:QBLK, 0:LANES] = zeros
        for r in range(4):
            k4[r * K4_RES_ROWS: r * K4_RES_ROWS + QBLK] = zeros
            v4[r * K4_RES_ROWS: r * K4_RES_ROWS + QBLK, 0:LANES] = zeros
        for r in range(16):
            k16[r * K16_RES_ROWS: r * K16_RES_ROWS + QBLK] = zeros
            v16[r * K16_RES_ROWS: r * K16_RES_ROWS + QBLK, 0:LANES] = zeros
        for ref in (v1, v4, v16):
            ref[:, LANES:2 * LANES] = jnp.ones((ref.shape[0], LANES), BF16)
        i = lax.broadcasted_iota(jnp.int32, (WIN, WIN), 0)
        j = lax.broadcasted_iota(jnp.int32, (WIN, WIN), 1)
        steps = QBLK + (i & (QBLK - 1)) - j
        valid = (steps >= 0) & (steps <= QBLK)
        slope = jnp.where(i < QBLK, slopes_ref[HEADS_PER_SLAB * slab],
                          slopes_ref[HEADS_PER_SLAB * slab + 1])
        for b, d in enumerate(DILATIONS):
            bias = jnp.where(valid, -slope * (steps * d).astype(F32), NEG)
            bias_ref[b, 1] = bias
            bias_ref[b, 0] = jnp.where(j >= QBLK, bias, NEG)

    k1[QBLK:K1_ROWS] = k_ref[0].astype(BF16)
    v1[QBLK:K1_ROWS, 0:LANES] = v_ref[0].astype(BF16)
    for r in range(4):
        for src, tmp in ((q_ref, aq), (k_ref, ak), (v_ref, av)):
            tmp[r] = src[0, pl.ds(r, SPAN // 4, stride=4), :]
        q4[r * (SPAN // 4):(r + 1) * (SPAN // 4)] = aq[r].astype(BF16)
        k4[r * K4_RES_ROWS + QBLK:(r + 1) * K4_RES_ROWS] = ak[r].astype(BF16)
        v4[r * K4_RES_ROWS + QBLK:(r + 1) * K4_RES_ROWS, 0:LANES] = av[r].astype(BF16)
        for r2 in range(4):
            r16 = 4 * r2 + r
            q16[r16 * QBLK:(r16 + 1) * QBLK] = aq[r, pl.ds(r2, QBLK, stride=4), :].astype(BF16)
            k16[r16 * K16_RES_ROWS + QBLK:(r16 + 1) * K16_RES_ROWS] = (
                ak[r, pl.ds(r2, QBLK, stride=4), :].astype(BF16))
            v16[r16 * K16_RES_ROWS + QBLK:(r16 + 1) * K16_RES_ROWS, 0:LANES] = (
                av[r, pl.ds(r2, QBLK, stride=4), :].astype(BF16))

    not_first = jnp.where(span == 0, 0, 1)

    def block(b, q, ks, vs, koff, sel):
        zero = jnp.zeros_like(q)
        qq = jnp.concatenate([jnp.where(head0, q, zero), jnp.where(head0, zero, q)], axis=0)
        kw = ks[pl.ds(pl.multiple_of(koff, QBLK), WIN), :]
        vw = vs[pl.ds(pl.multiple_of(koff, QBLK), WIN), :]
        s = lax.dot_general(qq, kw, (((1,), (1,)), ((), ())), preferred_element_type=F32)
        s = s + bias_ref[b, sel]
        m = jnp.max(s, axis=1, keepdims=True)
        p = jnp.exp(s - m).astype(BF16)
        acc = jnp.dot(p, vw, preferred_element_type=F32)
        o_t = jnp.where(head0, acc[0:QBLK, 0:LANES], acc[QBLK:WIN, 0:LANES])
        l_t = jnp.where(head0, acc[0:QBLK, LANES:], acc[QBLK:WIN, LANES:])
        m_t = jnp.where(head0, m[0:QBLK], m[QBLK:WIN])
        return o_t / l_t, m_t + jnp.log(l_t)

    def branch1(nb, c):
        sel = jnp.where(nb == 0, not_first, 1)
        start = pl.multiple_of(nb * QBLK, QBLK)
        q = q_ref[0, pl.ds(start, QBLK), :].astype(BF16)
        o_t, lse_t = block(0, q, k1, v1, start, sel)
        ob[0, pl.ds(start, QBLK), :] = o_t
        lb[0, pl.ds(start, QBLK), :] = lse_t
        return c

    def branch4(it, c):
        r = it // 4
        nb = it % 4
        sel = jnp.where(nb == 0, not_first, 1)
        q = q4[pl.ds(pl.multiple_of(it * QBLK, QBLK), QBLK), :]
        o_t, lse_t = block(1, q, k4, v4, r * K4_RES_ROWS + nb * QBLK, sel)
        start = nb * (4 * QBLK) + r
        ob[1, pl.ds(start, QBLK, stride=4), :] = o_t
        lb[1, pl.ds(start, QBLK, stride=4), :] = lse_t
        return c

    def branch16(r, c):
        q = q16[pl.ds(pl.multiple_of(r * QBLK, QBLK), QBLK), :]
        o_t, lse_t = block(2, q, k16, v16, r * K16_RES_ROWS, not_first)
        ob[2, pl.ds(r, QBLK, stride=16), :] = o_t
        lb[2, pl.ds(r, QBLK, stride=16), :] = lse_t
        return c

    lax.fori_loop(0, SPAN // QBLK, branch1, 0, unroll=BLOCK_UNROLL)
    lax.fori_loop(0, 16, branch4, 0, unroll=BLOCK_UNROLL)
    lax.fori_loop(0, 16, branch16, 0, unroll=BLOCK_UNROLL)

    def merge(c, carry):
        rows = pl.ds(pl.multiple_of(c * WIN, WIN), WIN)
        l0, l1, l2 = lb[0, rows, :], lb[1, rows, :], lb[2, rows, :]
        mx = jnp.maximum(jnp.maximum(l0, l1), l2)
        w0, w1, w2 = jnp.exp(l0 - mx), jnp.exp(l1 - mx), jnp.exp(l2 - mx)
        num = w0 * ob[0, rows, :] + w1 * ob[1, rows, :] + w2 * ob[2, rows, :]
        o_ref[rows, :] = (num / (w0 + w1 + w2)).astype(o_ref.dtype)
        return carry

    lax.fori_loop(0, SPAN // WIN, merge, 0)

    for ref, n_res, res_rows in ((k1, 1, K1_ROWS), (v1, 1, K1_ROWS), (k4, 4, K4_RES_ROWS),
                                 (v4, 4, K4_RES_ROWS), (k16, 16, K16_RES_ROWS), (v16, 16, K16_RES_ROWS)):
        for r in range(n_res):
            ref[r * res_rows: r * res_rows + QBLK, 0:LANES] = (
                ref[(r + 1) * res_rows - QBLK:(r + 1) * res_rows, 0:LANES])


def _attention(q, k, v, slopes, batch, seq):
    n_slabs, n, _ = q.shape
    assert seq % SPAN == 0
    spans = seq // SPAN
    in_spec = pl.BlockSpec((1, SPAN, LANES), lambda b, s, t: (s, b * spans + t, 0))
    return pl.pallas_call(
        _attn_kernel,
        grid=(batch, n_slabs, spans),
        in_specs=[pl.BlockSpec(memory_space=pltpu.SMEM), in_spec, in_spec, in_spec],
        out_specs=pl.BlockSpec((SPAN, LANES), lambda b, s, t: (b * spans + t, s)),
        out_shape=jax.ShapeDtypeStruct((n, n_slabs * LANES), BF16),
        scratch_shapes=[
            pltpu.VMEM((len(DILATIONS), 2, WIN, WIN), F32),
            pltpu.VMEM((SPAN, LANES), BF16),
            pltpu.VMEM((SPAN, LANES), BF16),
            pltpu.VMEM((K1_ROWS, LANES), BF16),
            pltpu.VMEM((K1_ROWS, 2 * LANES), BF16),
            pltpu.VMEM((4 * K4_RES_ROWS, LANES), BF16),
            pltpu.VMEM((4 * K4_RES_ROWS, 2 * LANES), BF16),
            pltpu.VMEM((16 * K16_RES_ROWS, LANES), BF16),
            pltpu.VMEM((16 * K16_RES_ROWS, 2 * LANES), BF16),
            pltpu.VMEM((4, SPAN // 4, LANES), F32),
            pltpu.VMEM((4, SPAN // 4, LANES), F32),
            pltpu.VMEM((4, SPAN // 4, LANES), F32),
            pltpu.VMEM((len(DILATIONS), SPAN, LANES), F32),
            pltpu.VMEM((len(DILATIONS), SPAN, LANES), F32),
        ],
        compiler_params=pltpu.CompilerParams(
            dimension_semantics=("arbitrary", "arbitrary", "arbitrary"),
            vmem_limit_bytes=VMEM_LIMIT),
        name="attn",
    )(slopes, q, k, v)


def _tail_kernel(x_ref, attn_ref, sgu_ref, p_ref, ga_ref, wout_ref, gpm_ref, gpf_ref,
                 wgu_ref, wd_ref, gpo_ref, wpg_ref, bpg_ref, wpp_ref, o_ref):
    d_ff = wd_ref.shape[0]
    an = _rms(attn_ref[...].astype(F32), ga_ref[...]).astype(BF16)
    groups = jnp.concatenate([an, sgu_ref[...]], axis=-1)
    mixed = jnp.dot(groups, wout_ref[...], preferred_element_type=F32)
    h = x_ref[...] + _rms(mixed, gpm_ref[...])
    f = _rms(h, gpf_ref[...]).astype(BF16)
    gu = jnp.dot(f, wgu_ref[...], preferred_element_type=F32)
    hid = (jax.nn.silu(gu[:, :d_ff]) * gu[:, d_ff:]).astype(BF16)
    y = jnp.dot(hid, wd_ref[...], preferred_element_type=F32)
    h = h + _rms(y, gpo_ref[...])
    gate = jax.nn.sigmoid(jnp.dot(h.astype(BF16), wpg_ref[...], preferred_element_type=F32)
                          + bpg_ref[...])
    pe = jnp.dot(p_ref[...].astype(BF16), wpp_ref[...], preferred_element_type=F32)
    o_ref[...] = h + gate * pe


def _tail(x2, attn, sgu, p2, *params):
    n, d = x2.shape
    tm = TAIL_TILE

    def tile(a):
        return pl.BlockSpec((tm, a.shape[1]), lambda i: (i, 0))

    return pl.pallas_call(
        _tail_kernel,
        grid=(n // tm,),
        in_specs=[tile(x2), tile(attn), tile(sgu), tile(p2)] + [_const_spec(a.shape) for a in params],
        out_specs=pl.BlockSpec((tm, d), lambda i: (i, 0)),
        out_shape=jax.ShapeDtypeStruct((n, d), F32),
        compiler_params=pltpu.CompilerParams(dimension_semantics=("arbitrary",),
                                             vmem_limit_bytes=VMEM_LIMIT),
        name="tail",
    )(x2, attn, sgu, p2, *params)


def kernel(x, p, ln_pre_mix, w_in, sgu_ln_g, sgu_ln_b, w_spatial, b_spatial, attn_out_norm,
           sgu_out_norm, w_out, ln_post_mix, ln_pre_ffn, w_gate_up, w_down, ln_post_ffn,
           w_pe_gate, b_pe_gate, w_pe_proj):
    batch, seq, d = x.shape
    depth = w_in.shape[0]
    attn_w = attn_out_norm.shape[1]
    sgu_w = sgu_out_norm.shape[1]
    n_heads = attn_w // HEAD_DIM
    slopes = 2.0 ** (-8.0 * (jnp.arange(n_heads, dtype=F32) + 1.0) / n_heads)

    def row(a):
        return a.reshape(1, -1)

    h = x.reshape(batch * seq, d)
    for i in range(depth):
        q, k, v, sgu = _proj_sgu(
            h, row(ln_pre_mix[i]), w_in[i].astype(BF16), row(sgu_ln_g[i]), row(sgu_ln_b[i]),
            w_spatial[i], b_spatial[i][:, :, None], row(sgu_out_norm[i]), attn_w, sgu_w)
        attn = _attention(q, k, v, slopes, batch, seq)
        h = _tail(h, attn, sgu, p[i].reshape(batch * seq, -1),
                  row(attn_out_norm[i]), w_out[i].astype(BF16), row(ln_post_mix[i]),
                  row(ln_pre_ffn[i]), w_gate_up[i].astype(BF16), w_down[i].astype(BF16),
                  row(ln_post_ffn[i]), w_pe_gate[i].astype(BF16), row(b_pe_gate[i]),
                  w_pe_proj[i].astype(BF16))
    return h.reshape(batch, seq, d)
```

```python
import functools

import jax
import jax.numpy as jnp
from jax import lax
from jax.experimental import pallas as pl
from jax.experimental.pallas import tpu as pltpu

F32 = jnp.float32
BF16 = jnp.bfloat16

HEAD_DIM = 64
QBLK = 128
WIN = 2 * QBLK
DILATIONS = (1, 4, 16)
SPAN = QBLK * DILATIONS[-1]
SGU_CHUNK = 128
N_SGU_GROUPS = 4
EPS = 1e-6
NEG = -1e30

LANES = 128
HEADS_PER_SLAB = LANES // HEAD_DIM
BLOCK_UNROLL = 16

PROJ_TILE = 512
TAIL_TILE = 512
TAIL_CHAINS = 1
VMEM_LIMIT = 56 * 1024 * 1024


def _rms(x, g):
    return x * lax.rsqrt(jnp.mean(x * x, axis=-1, keepdims=True) + EPS) * g


def _const_spec(shape):
    zeros = (0,) * len(shape)
    return pl.BlockSpec(shape, lambda *_: zeros, pipeline_mode=pl.Buffered(1))


def _proj_sgu_kernel(x_ref, g_ref, w_ref, lng_ref, lnb_ref, ws_ref, bs_ref, gout_ref,
                     q_ref, k_ref, v_ref, sgu_ref, *, attn_w, sgu_w):
    tm = x_ref.shape[0]
    gd = sgu_w // N_SGU_GROUPS
    a = _rms(x_ref[...], g_ref[...]).astype(BF16)

    def project(c0, width):
        return jnp.dot(a, w_ref[:, c0:c0 + width], preferred_element_type=F32)

    def emit_slabs(ref, cols):
        for s in range(attn_w // LANES):
            ref[s] = cols[:, s * LANES:(s + 1) * LANES]

    row = lax.broadcasted_iota(jnp.int32, (SGU_CHUNK, SGU_CHUNK), 0)
    col = lax.broadcasted_iota(jnp.int32, (SGU_CHUNK, SGU_CHUNK), 1)
    causal = row >= col
    outs = []
    ssq = jnp.zeros((tm, 1), F32)
    uz_pairs = [project(0, 4 * gd)]
    for g in range(N_SGU_GROUPS):
        uz = uz_pairs[g // 2][:, (g % 2) * 2 * gd:(g % 2 + 1) * 2 * gd]
        if g == 0:
            uz_pairs.append(project(4 * gd, 4 * gd))
        elif g == 1:
            emit_slabs(q_ref, project(2 * sgu_w, attn_w) * (HEAD_DIM ** -0.5))
        elif g == 2:
            emit_slabs(k_ref, project(2 * sgu_w + attn_w, attn_w))
        elif g == 3:
            emit_slabs(v_ref, project(2 * sgu_w + 2 * attn_w, attn_w))
        u = jax.nn.gelu(uz[:, :gd])
        z = jax.nn.gelu(uz[:, gd:])
        zc = z - jnp.mean(z, axis=-1, keepdims=True)
        zn = zc * lax.rsqrt(jnp.mean(zc * zc, axis=-1, keepdims=True) + EPS)
        zn = (zn * lng_ref[...] + lnb_ref[...]).astype(BF16)
        wm = jnp.where(causal, ws_ref[g], 0.0).astype(BF16)
        mixed = [jnp.dot(wm, zn[c * SGU_CHUNK:(c + 1) * SGU_CHUNK], preferred_element_type=F32)
                 + bs_ref[g] for c in range(tm // SGU_CHUNK)]
        o = u * jnp.concatenate(mixed, axis=0)
        ssq = ssq + jnp.sum(o * o, axis=-1, keepdims=True)
        outs.append(o)
    scale = lax.rsqrt(ssq / sgu_w + EPS)
    for g in range(N_SGU_GROUPS):
        sgu_ref[:, g * gd:(g + 1) * gd] = (outs[g] * scale * gout_ref[:, g * gd:(g + 1) * gd]).astype(BF16)


def _proj_sgu(x2, g_pre, w_in, ln_g, ln_b, w_sp, b_sp, g_sgu, attn_w, sgu_w):
    n, d = x2.shape
    tm = PROJ_TILE
    n_slabs = attn_w // LANES
    slab = jax.ShapeDtypeStruct((n_slabs, n, LANES), F32)
    slab_spec = pl.BlockSpec((n_slabs, tm, LANES), lambda i: (0, i, 0))
    return pl.pallas_call(
        functools.partial(_proj_sgu_kernel, attn_w=attn_w, sgu_w=sgu_w),
        grid=(n // tm,),
        in_specs=[
            pl.BlockSpec((tm, d), lambda i: (i, 0)),
            _const_spec(g_pre.shape), _const_spec(w_in.shape), _const_spec(ln_g.shape),
            _const_spec(ln_b.shape), _const_spec(w_sp.shape), _const_spec(b_sp.shape),
            _const_spec(g_sgu.shape),
        ],
        out_specs=[slab_spec, slab_spec, slab_spec, pl.BlockSpec((tm, sgu_w), lambda i: (i, 0))],
        out_shape=[slab, slab, slab, jax.ShapeDtypeStruct((n, sgu_w), BF16)],
        compiler_params=pltpu.CompilerParams(dimension_semantics=("arbitrary",),
                                             vmem_limit_bytes=VMEM_LIMIT),
        name="proj_sgu",
    )(x2, g_pre, w_in, ln_g, ln_b, w_sp, b_sp, g_sgu)


K1_ROWS = QBLK + SPAN
K4_RES_ROWS = QBLK + SPAN // 4
K16_RES_ROWS = WIN


def _attn_kernel(slopes_ref, q_ref, k_ref, v_ref, o_ref,
                 bias_ref, q4, q16, k1, v1, k4, v4, k16, v16, aq, ak, av, ob, lb):
    slab = pl.program_id(1)
    span = pl.program_id(2)
    lane = lax.broadcasted_iota(jnp.int32, (QBLK, LANES), 1)
    head0 = lane < HEAD_DIM

    @pl.when(span == 0)
    def _start_of_sequence():
        zeros = jnp.zeros((QBLK, LANES), BF16)
        k1[0:QBLK] = zeros
        v1[0:QBLK, 0:LANES] = zeros
        for r in range(4):
            k4[r * K4_RES_ROWS: r * K4_RES_ROWS + QBLK] = zeros
            v4[r * K4_RES_ROWS: r * K4_RES_ROWS + QBLK, 0:LANES] = zeros
        for r in range(16):
            k16[r * K16_RES_ROWS: r * K16_RES_ROWS + QBLK] = zeros
            v16[r * K16_RES_ROWS: r * K16_RES_ROWS + QBLK, 0:LANES] = zeros
        for ref in (v1, v4, v16):
            ref[:, LANES:2 * LANES] = jnp.ones((ref.shape[0], LANES), BF16)
        i = lax.broadcasted_iota(jnp.int32, (WIN, WIN), 0)
        j = lax.broadcasted_iota(jnp.int32, (WIN, WIN), 1)
        steps = QBLK + (i & (QBLK - 1)) - j
        valid = (steps >= 0) & (steps <= QBLK)
        slope = jnp.where(i < QBLK, slopes_ref[HEADS_PER_SLAB * slab],
                          slopes_ref[HEADS_PER_SLAB * slab + 1])
        for b, d in enumerate(DILATIONS):
            bias = jnp.where(valid, -slope * (steps * d).astype(F32), NEG)
            bias_ref[b, 1] = bias
            bias_ref[b, 0] = jnp.where(j >= QBLK, bias, NEG)

    k1[QBLK:K1_ROWS] = k_ref[0].astype(BF16)
    v1[QBLK:K1_ROWS, 0:LANES] = v_ref[0].astype(BF16)
    for r in range(4):
        for src, tmp in ((q_ref, aq), (k_ref, ak), (v_ref, av)):
            tmp[r] = src[0, pl.ds(r, SPAN // 4, stride=4), :]
        q4[r * (SPAN // 4):(r + 1) * (SPAN // 4)] = aq[r].astype(BF16)
        k4[r * K4_RES_ROWS + QBLK:(r + 1) * K4_RES_ROWS] = ak[r].astype(BF16)
        v4[r * K4_RES_ROWS + QBLK:(r + 1) * K4_RES_ROWS, 0:LANES] = av[r].astype(BF16)
        for r2 in range(4):
            r16 = 4 * r2 + r
            q16[r16 * QBLK:(r16 + 1) * QBLK] = aq[r, pl.ds(r2, QBLK, stride=4), :].astype(BF16)
            k16[r16 * K16_RES_ROWS + QBLK:(r16 + 1) * K16_RES_ROWS] = (
                ak[r, pl.ds(r2, QBLK, stride=4), :].astype(BF16))
            v16[r16 * K16_RES_ROWS + QBLK:(r16 + 1) * K16_RES_ROWS, 0:LANES] = (
                av[r, pl.ds(r2, QBLK, stride=4), :].astype(BF16))

    not_first = jnp.where(span == 0, 0, 1)

    def block(b, q, ks, vs, koff, sel):
        zero = jnp.zeros_like(q)
        qq = jnp.concatenate([jnp.where(head0, q, zero), jnp.where(head0, zero, q)], axis=0)
        kw = ks[pl.ds(pl.multiple_of(koff, QBLK), WIN), :]
        vw = vs[pl.ds(pl.multiple_of(koff, QBLK), WIN), :]
        s = lax.dot_general(qq, kw, (((1,), (1,)), ((), ())), preferred_element_type=F32)
        s = s + bias_ref[b, sel]
        m = jnp.max(s, axis=1, keepdims=True)
        p = jnp.exp(s - m).astype(BF16)
        acc = jnp.dot(p, vw, preferred_element_type=F32)
        o_t = jnp.where(head0, acc[0:QBLK, 0:LANES], acc[QBLK:WIN, 0:LANES])
        l_t = jnp.where(head0, acc[0:QBLK, LANES:], acc[QBLK:WIN, LANES:])
        m_t = jnp.where(head0, m[0:QBLK], m[QBLK:WIN])
        return o_t / l_t, m_t + jnp.log(l_t)

    def branch1(nb, c):
        sel = jnp.where(nb == 0, not_first, 1)
        start = pl.multiple_of(nb * QBLK, QBLK)
        q = q_ref[0, pl.ds(start, QBLK), :].astype(BF16)
        o_t, lse_t = block(0, q, k1, v1, start, sel)
        ob[0, pl.ds(start, QBLK), :] = o_t
        lb[0, pl.ds(start, QBLK), :] = lse_t
        return c

    def branch4(it, c):
        r = it // 4
        nb = it % 4
        sel = jnp.where(nb == 0, not_first, 1)
        q = q4[pl.ds(pl.multiple_of(it * QBLK, QBLK), QBLK), :]
        o_t, lse_t = block(1, q, k4, v4, r * K4_RES_ROWS + nb * QBLK, sel)
        start = nb * (4 * QBLK) + r
        ob[1, pl.ds(start, QBLK, stride=4), :] = o_t
        lb[1, pl.ds(start, QBLK, stride=4), :] = lse_t
        return c

    def branch16(r, c):
        q = q16[pl.ds(pl.multiple_of(r * QBLK, QBLK), QBLK), :]
        o_t, lse_t = block(2, q, k16, v16, r * K16_RES_ROWS, not_first)
        ob[2, pl.ds(r, QBLK, stride=16), :] = o_t
        lb[2, pl.ds(r, QBLK, stride=16), :] = lse_t
        return c

    lax.fori_loop(0, SPAN // QBLK, branch1, 0, unroll=BLOCK_UNROLL)
    lax.fori_loop(0, 16, branch4, 0, unroll=BLOCK_UNROLL)
    lax.fori_loop(0, 16, branch16, 0, unroll=BLOCK_UNROLL)

    def merge(c, carry):
        rows = pl.ds(pl.multiple_of(c * WIN, WIN), WIN)
        l0, l1, l2 = lb[0, rows, :], lb[1, rows, :], lb[2, rows, :]
        mx = jnp.maximum(jnp.maximum(l0, l1), l2)
        w0, w1, w2 = jnp.exp(l0 - mx), jnp.exp(l1 - mx), jnp.exp(l2 - mx)
        num = w0 * ob[0, rows, :] + w1 * ob[1, rows, :] + w2 * ob[2, rows, :]
        o_ref[rows, :] = (num / (w0 + w1 + w2)).astype(o_ref.dtype)
        return carry

    lax.fori_loop(0, SPAN // WIN, merge, 0)

    for ref, n_res, res_rows in ((k1, 1, K1_ROWS), (v1, 1, K1_ROWS), (k4, 4, K4_RES_ROWS),
                                 (v4, 4, K4_RES_ROWS), (k16, 16, K16_RES_ROWS), (v16, 16, K16_RES_ROWS)):
        for r in range(n_res):
            ref[r * res_rows: r * res_rows + QBLK, 0:LANES] = (
                ref[(r + 1) * res_rows - QBLK:(r + 1) * res_rows, 0:LANES])


def _attention(q, k, v, slopes, batch, seq):
    n_slabs, n, _ = q.shape
    assert seq % SPAN == 0
    spans = seq // SPAN
    in_spec = pl.BlockSpec((1, SPAN, LANES), lambda b, s, t: (s, b * spans + t, 0))
    return pl.pallas_call(
        _attn_kernel,
        grid=(batch, n_slabs, spans),
        in_specs=[pl.BlockSpec(memory_space=pltpu.SMEM), in_spec, in_spec, in_spec],
        out_specs=pl.BlockSpec((SPAN, LANES), lambda b, s, t: (b * spans + t, s)),
        out_shape=jax.ShapeDtypeStruct((n, n_slabs * LANES), BF16),
        scratch_shapes=[
            pltpu.VMEM((len(DILATIONS), 2, WIN, WIN), F32),
            pltpu.VMEM((SPAN, LANES), BF16),
            pltpu.VMEM((SPAN, LANES), BF16),
            pltpu.VMEM((K1_ROWS, LANES), BF16),
            pltpu.VMEM((K1_ROWS, 2 * LANES), BF16),
            pltpu.VMEM((4 * K4_RES_ROWS, LANES), BF16),
            pltpu.VMEM((4 * K4_RES_ROWS, 2 * LANES), BF16),
            pltpu.VMEM((16 * K16_RES_ROWS, LANES), BF16),
            pltpu.VMEM((16 * K16_RES_ROWS, 2 * LANES), BF16),
            pltpu.VMEM((4, SPAN // 4, LANES), F32),
            pltpu.VMEM((4, SPAN // 4, LANES), F32),
            pltpu.VMEM((4, SPAN // 4, LANES), F32),
            pltpu.VMEM((len(DILATIONS), SPAN, LANES), F32),
            pltpu.VMEM((len(DILATIONS), SPAN, LANES), F32),
        ],
        compiler_params=pltpu.CompilerParams(
            dimension_semantics=("arbitrary", "arbitrary", "arbitrary"),
            vmem_limit_bytes=VMEM_LIMIT),
        name="attn",
    )(slopes, q, k, v)


def _tail_kernel(x_ref, attn_ref, sgu_ref, p_ref, ga_ref, wout_ref, gpm_ref, gpf_ref,
                 wgu_ref, wd_ref, gpo_ref, wpg_ref, bpg_ref, wpp_ref, o_ref):
    d_ff = wd_ref.shape[0]
    sub = x_ref.shape[0] // TAIL_CHAINS
    for c in range(TAIL_CHAINS):
        rows = slice(c * sub, (c + 1) * sub)
        an = _rms(attn_ref[rows, :].astype(F32), ga_ref[...]).astype(BF16)
        groups = jnp.concatenate([an, sgu_ref[rows, :]], axis=-1)
        mixed = jnp.dot(groups, wout_ref[...], preferred_element_type=F32)
        h = x_ref[rows, :] + _rms(mixed, gpm_ref[...])
        f = _rms(h, gpf_ref[...]).astype(BF16)
        gu = jnp.dot(f, wgu_ref[...], preferred_element_type=F32)
        hid = (jax.nn.silu(gu[:, :d_ff]) * gu[:, d_ff:]).astype(BF16)
        y = jnp.dot(hid, wd_ref[...], preferred_element_type=F32)
        h = h + _rms(y, gpo_ref[...])
        gate = jax.nn.sigmoid(jnp.dot(h.astype(BF16), wpg_ref[...], preferred_element_type=F32)
                              + bpg_ref[...])
        pe = jnp.dot(p_ref[rows, :].astype(BF16), wpp_ref[...], preferred_element_type=F32)
        o_ref[rows, :] = h + gate * pe


def _tail(x2, attn, sgu, p2, *params):
    n, d = x2.shape
    tm = TAIL_TILE

    def tile(a):
        return pl.BlockSpec((tm, a.shape[1]), lambda i: (i, 0))

    return pl.pallas_call(
        _tail_kernel,
        grid=(n // tm,),
        in_specs=[tile(x2), tile(attn), tile(sgu), tile(p2)] + [_const_spec(a.shape) for a in params],
        out_specs=pl.BlockSpec((tm, d), lambda i: (i, 0)),
        out_shape=jax.ShapeDtypeStruct((n, d), F32),
        compiler_params=pltpu.CompilerParams(dimension_semantics=("arbitrary",),
                                             vmem_limit_bytes=VMEM_LIMIT),
        name="tail",
    )(x2, attn, sgu, p2, *params)


def _group_major(w_in, attn_w, sgu_w):
    gd = sgu_w // N_SGU_GROUPS
    u0 = 3 * attn_w
    z0 = u0 + sgu_w
    cols = []
    for g in range(N_SGU_GROUPS):
        cols += [w_in[:, u0 + g * gd:u0 + (g + 1) * gd], w_in[:, z0 + g * gd:z0 + (g + 1) * gd]]
    return jnp.concatenate(cols + [w_in[:, :u0]], axis=1)


def kernel(x, p, ln_pre_mix, w_in, sgu_ln_g, sgu_ln_b, w_spatial, b_spatial, attn_out_norm,
           sgu_out_norm, w_out, ln_post_mix, ln_pre_ffn, w_gate_up, w_down, ln_post_ffn,
           w_pe_gate, b_pe_gate, w_pe_proj):
    batch, seq, d = x.shape
    depth = w_in.shape[0]
    attn_w = attn_out_norm.shape[1]
    sgu_w = sgu_out_norm.shape[1]
    n_heads = attn_w // HEAD_DIM
    slopes = 2.0 ** (-8.0 * (jnp.arange(n_heads, dtype=F32) + 1.0) / n_heads)

    def row(a):
        return a.reshape(1, -1)

    h = x.reshape(batch * seq, d)
    for i in range(depth):
        q, k, v, sgu = _proj_sgu(
            h, row(ln_pre_mix[i]), _group_major(w_in[i], attn_w, sgu_w).astype(BF16),
            row(sgu_ln_g[i]), row(sgu_ln_b[i]),
            w_spatial[i], b_spatial[i][:, :, None], row(sgu_out_norm[i]), attn_w, sgu_w)
        attn = _attention(q, k, v, slopes, batch, seq)
        h = _tail(h, attn, sgu, p[i].reshape(batch * seq, -1),
                  row(attn_out_norm[i]), w_out[i].astype(BF16), row(ln_post_mix[i]),
                  row(ln_pre_ffn[i]), w_gate_up[i].astype(BF16), w_down[i].astype(BF16),
                  row(ln_post_ffn[i]), w_pe_gate[i].astype(BF16), row(b_pe_gate[i]),
                  w_pe_proj[i].astype(BF16))
    return h.reshape(batch, seq, d)
```

```python
import functools

import jax
import jax.numpy as jnp
from jax import lax
from jax.experimental import pallas as pl
from jax.experimental.pallas import tpu as pltpu

F32 = jnp.float32
BF16 = jnp.bfloat16

HEAD_DIM = 64
QBLK = 128
WIN = 2 * QBLK
DILATIONS = (1, 4, 16)
SPAN = QBLK * DILATIONS[-1]
SGU_CHUNK = 128
N_SGU_GROUPS = 4
EPS = 1e-6
NEG = -1e30

LANES = 128
HEADS_PER_SLAB = LANES // HEAD_DIM
BLOCK_UNROLL = 16

PROJ_TILE = 512
TAIL_TILE = 512
TAIL_CHAINS = 1
VMEM_LIMIT = 56 * 1024 * 1024


def _rms(x, g):
    return x * lax.rsqrt(jnp.mean(x * x, axis=-1, keepdims=True) + EPS) * g


def _const_spec(shape):
    zeros = (0,) * len(shape)
    return pl.BlockSpec(shape, lambda *_: zeros, pipeline_mode=pl.Buffered(1))


def _proj_sgu_kernel(x_ref, g_ref, w_ref, lng_ref, lnb_ref, ws_ref, bs_ref, gout_ref,
                     q_ref, k_ref, v_ref, sgu_ref, *, attn_w, sgu_w):
    tm = x_ref.shape[0]
    gd = sgu_w // N_SGU_GROUPS
    a = _rms(x_ref[...], g_ref[...]).astype(BF16)

    def project(c0, width):
        return jnp.dot(a, w_ref[:, c0:c0 + width], preferred_element_type=F32)

    def emit_slabs(ref, cols):
        for s in range(attn_w // LANES):
            ref[s] = cols[:, s * LANES:(s + 1) * LANES]

    row = lax.broadcasted_iota(jnp.int32, (SGU_CHUNK, SGU_CHUNK), 0)
    col = lax.broadcasted_iota(jnp.int32, (SGU_CHUNK, SGU_CHUNK), 1)
    causal = row >= col
    outs = []
    ssq = jnp.zeros((tm, 1), F32)
    uz_pairs = [project(0, 4 * gd)]
    for g in range(N_SGU_GROUPS):
        uz = uz_pairs[g // 2][:, (g % 2) * 2 * gd:(g % 2 + 1) * 2 * gd]
        if g == 0:
            uz_pairs.append(project(4 * gd, 4 * gd))
        elif g == 1:
            emit_slabs(q_ref, project(2 * sgu_w, attn_w) * (HEAD_DIM ** -0.5))
        elif g == 2:
            emit_slabs(k_ref, project(2 * sgu_w + attn_w, attn_w))
        elif g == 3:
            emit_slabs(v_ref, project(2 * sgu_w + 2 * attn_w, attn_w))
        u = jax.nn.gelu(uz[:, :gd])
        z = jax.nn.gelu(uz[:, gd:])
        zc = z - jnp.mean(z, axis=-1, keepdims=True)
        zn = zc * lax.rsqrt(jnp.mean(zc * zc, axis=-1, keepdims=True) + EPS)
        zn = (zn * lng_ref[...] + lnb_ref[...]).astype(BF16)
        wm = jnp.where(causal, ws_ref[g], 0.0).astype(BF16)
        mixed = [jnp.dot(wm, zn[c * SGU_CHUNK:(c + 1) * SGU_CHUNK], preferred_element_type=F32)
                 + bs_ref[g] for c in range(tm // SGU_CHUNK)]
        o = u * jnp.concatenate(mixed, axis=0)
        ssq = ssq + jnp.sum(o * o, axis=-1, keepdims=True)
        outs.append(o)
    scale = lax.rsqrt(ssq / sgu_w + EPS)
    for g in range(N_SGU_GROUPS):
        sgu_ref[:, g * gd:(g + 1) * gd] = (outs[g] * scale * gout_ref[:, g * gd:(g + 1) * gd]).astype(BF16)


def _proj_sgu(x2, g_pre, w_in, ln_g, ln_b, w_sp, b_sp, g_sgu, attn_w, sgu_w):
    n, d = x2.shape
    tm = PROJ_TILE
    n_slabs = attn_w // LANES
    slab = jax.ShapeDtypeStruct((n_slabs, n, LANES), F32)
    slab_spec = pl.BlockSpec((n_slabs, tm, LANES), lambda i: (0, i, 0))
    return pl.pallas_call(
        functools.partial(_proj_sgu_kernel, attn_w=attn_w, sgu_w=sgu_w),
        grid=(n // tm,),
        in_specs=[
            pl.BlockSpec((tm, d), lambda i: (i, 0)),
            _const_spec(g_pre.shape), _const_spec(w_in.shape), _const_spec(ln_g.shape),
            _const_spec(ln_b.shape), _const_spec(w_sp.shape), _const_spec(b_sp.shape),
            _const_spec(g_sgu.shape),
        ],
        out_specs=[slab_spec, slab_spec, slab_spec, pl.BlockSpec((tm, sgu_w), lambda i: (i, 0))],
        out_shape=[slab, slab, slab, jax.ShapeDtypeStruct((n, sgu_w), BF16)],
        compiler_params=pltpu.CompilerParams(dimension_semantics=("arbitrary",),
                                             vmem_limit_bytes=VMEM_LIMIT),
        name="proj_sgu",
    )(x2, g_pre, w_in, ln_g, ln_b, w_sp, b_sp, g_sgu)


K1_ROWS = QBLK + SPAN
K4_RES_ROWS = QBLK + SPAN // 4
K16_RES_ROWS = WIN


def _attn_kernel(slopes_ref, q_ref, k_ref, v_ref, o_ref,
                 bias_ref, q4, q16, k1, v1, k4, v4, k16, v16, aq, ak, av, ob, lb):
    slab = pl.program_id(1)
    span = pl.program_id(2)
    lane = lax.broadcasted_iota(jnp.int32, (QBLK, LANES), 1)
    head0 = lane < HEAD_DIM

    @pl.when(span == 0)
    def _start_of_sequence():
        zeros = jnp.zeros((QBLK, LANES), BF16)
        k1[0:QBLK] = zeros
        v1[0:QBLK, 0:LANES] = zeros
        for r in range(4):
            k4[r * K4_RES_ROWS: r * K4_RES_ROWS + QBLK] = zeros
            v4[r * K4_RES_ROWS: r * K4_RES_ROWS + QBLK, 0:LANES] = zeros
        for r in range(16):
            k16[r * K16_RES_ROWS: r * K16_RES_ROWS + QBLK] = zeros
            v16[r * K16_RES_ROWS: r * K16_RES_ROWS + QBLK, 0:LANES] = zeros
        for ref in (v1, v4, v16):
            ref[:, LANES:2 * LANES] = jnp.ones((ref.shape[0], LANES), BF16)
        i = lax.broadcasted_iota(jnp.int32, (WIN, WIN), 0)
        j = lax.broadcasted_iota(jnp.int32, (WIN, WIN), 1)
        steps = QBLK + (i & (QBLK - 1)) - j
        valid = (steps >= 0) & (steps <= QBLK)
        slope = jnp.where(i < QBLK, slopes_ref[HEADS_PER_SLAB * slab],
                          slopes_ref[HEADS_PER_SLAB * slab + 1])
        for b, d in enumerate(DILATIONS):
            bias = jnp.where(valid, -slope * (steps * d).astype(F32), NEG)
            bias_ref[b, 1] = bias
            bias_ref[b, 0] = jnp.where(j >= QBLK, bias, NEG)

    k1[QBLK:K1_ROWS] = k_ref[0].astype(BF16)
    v1[QBLK:K1_ROWS, 0:LANES] = v_ref[0].astype(BF16)
    for r in range(4):
        for src, tmp in ((q_ref, aq), (k_ref, ak), (v_ref, av)):
            tmp[r] = src[0, pl.ds(r, SPAN // 4, stride=4), :]
        q4[r * (SPAN // 4):(r + 1) * (SPAN // 4)] = aq[r].astype(BF16)
        k4[r * K4_RES_ROWS + QBLK:(r + 1) * K4_RES_ROWS] = ak[r].astype(BF16)
        v4[r * K4_RES_ROWS + QBLK:(r + 1) * K4_RES_ROWS, 0:LANES] = av[r].astype(BF16)
        for r2 in range(4):
            r16 = 4 * r2 + r
            q16[r16 * QBLK:(r16 + 1) * QBLK] = aq[r, pl.ds(r2, QBLK, stride=4), :].astype(BF16)
            k16[r16 * K16_RES_ROWS + QBLK:(r16 + 1) * K16_RES_ROWS] = (
                ak[r, pl.ds(r2, QBLK, stride=4), :].astype(BF16))
            v16[r16 * K16_RES_ROWS + QBLK:(r16 + 1) * K16_RES_ROWS, 0:LANES] = (
                av[r, pl.ds(r2, QBLK, stride=4), :].astype(BF16))

    not_first = jnp.where(span == 0, 0, 1)

    def block(b, q, ks, vs, koff, sel):
        zero = jnp.zeros_like(q)
        qq = jnp.concatenate([jnp.where(head0, q, zero), jnp.where(head0, zero, q)], axis=0)
        kw = ks[pl.ds(pl.multiple_of(koff, QBLK), WIN), :]
        vw = vs[pl.ds(pl.multiple_of(koff, QBLK), WIN), :]
        s = lax.dot_general(qq, kw, (((1,), (1,)), ((), ())), preferred_element_type=F32)
        s = s + bias_ref[b, sel]
        m = jnp.max(s, axis=1, keepdims=True)
        p = jnp.exp(s - m).astype(BF16)
        acc = jnp.dot(p, vw, preferred_element_type=F32)
        o_t = jnp.where(head0, acc[0:QBLK, 0:LANES], acc[QBLK:WIN, 0:LANES])
        l_t = jnp.where(head0, acc[0:QBLK, LANES:], acc[QBLK:WIN, LANES:])
        m_t = jnp.where(head0, m[0:QBLK], m[QBLK:WIN])
        return o_t / l_t, m_t + jnp.log(l_t)

    def branch1(nb, c):
        sel = jnp.where(nb == 0, not_first, 1)
        start = pl.multiple_of(nb * QBLK, QBLK)
        q = q_ref[0, pl.ds(start, QBLK), :].astype(BF16)
        o_t, lse_t = block(0, q, k1, v1, start, sel)
        ob[0, pl.ds(start, QBLK), :] = o_t
        lb[0, pl.ds(start, QBLK), :] = lse_t
        return c

    def branch4(it, c):
        r = it // 4
        nb = it % 4
        sel = jnp.where(nb == 0, not_first, 1)
        q = q4[pl.ds(pl.multiple_of(it * QBLK, QBLK), QBLK), :]
        o_t, lse_t = block(1, q, k4, v4, r * K4_RES_ROWS + nb * QBLK, sel)
        start = pl.multiple_of(it * QBLK, QBLK)
        ob[1, pl.ds(start, QBLK), :] = o_t
        lb[1, pl.ds(start, QBLK), :] = lse_t
        return c

    def branch16(r, c):
        q = q16[pl.ds(pl.multiple_of(r * QBLK, QBLK), QBLK), :]
        o_t, lse_t = block(2, q, k16, v16, r * K16_RES_ROWS, not_first)
        start = (r % 4) * (SPAN // 4) + r // 4
        ob[2, pl.ds(start, QBLK, stride=4), :] = o_t
        lb[2, pl.ds(start, QBLK, stride=4), :] = lse_t
        return c

    lax.fori_loop(0, SPAN // QBLK, branch1, 0, unroll=BLOCK_UNROLL)
    lax.fori_loop(0, 16, branch4, 0, unroll=BLOCK_UNROLL)
    lax.fori_loop(0, 16, branch16, 0, unroll=BLOCK_UNROLL)

    def merge(c, carry):
        rows = pl.ds(pl.multiple_of(c * WIN, WIN), WIN)
        tokens = pl.ds(c // 2 + (c % 2) * (4 * WIN), WIN, stride=4)
        l0, l1, l2 = lb[0, tokens, :], lb[1, rows, :], lb[2, rows, :]
        mx = jnp.maximum(jnp.maximum(l0, l1), l2)
        w0, w1, w2 = jnp.exp(l0 - mx), jnp.exp(l1 - mx), jnp.exp(l2 - mx)
        num = w0 * ob[0, tokens, :] + w1 * ob[1, rows, :] + w2 * ob[2, rows, :]
        o_ref[0, tokens, :] = num / (w0 + w1 + w2)
        return carry

    lax.fori_loop(0, SPAN // WIN, merge, 0)

    for ref, n_res, res_rows in ((k1, 1, K1_ROWS), (v1, 1, K1_ROWS), (k4, 4, K4_RES_ROWS),
                                 (v4, 4, K4_RES_ROWS), (k16, 16, K16_RES_ROWS), (v16, 16, K16_RES_ROWS)):
        for r in range(n_res):
            ref[r * res_rows: r * res_rows + QBLK, 0:LANES] = (
                ref[(r + 1) * res_rows - QBLK:(r + 1) * res_rows, 0:LANES])


def _attention(q, k, v, slopes, batch, seq):
    n_slabs, n, _ = q.shape
    assert seq % SPAN == 0
    spans = seq // SPAN
    in_spec = pl.BlockSpec((1, SPAN, LANES), lambda b, s, t: (s, b * spans + t, 0))
    return pl.pallas_call(
        _attn_kernel,
        grid=(batch, n_slabs, spans),
        in_specs=[pl.BlockSpec(memory_space=pltpu.SMEM), in_spec, in_spec, in_spec],
        out_specs=in_spec,
        out_shape=jax.ShapeDtypeStruct((n_slabs, n, LANES), F32),
        scratch_shapes=[
            pltpu.VMEM((len(DILATIONS), 2, WIN, WIN), F32),
            pltpu.VMEM((SPAN, LANES), BF16),
            pltpu.VMEM((SPAN, LANES), BF16),
            pltpu.VMEM((K1_ROWS, LANES), BF16),
            pltpu.VMEM((K1_ROWS, 2 * LANES), BF16),
            pltpu.VMEM((4 * K4_RES_ROWS, LANES), BF16),
            pltpu.VMEM((4 * K4_RES_ROWS, 2 * LANES), BF16),
            pltpu.VMEM((16 * K16_RES_ROWS, LANES), BF16),
            pltpu.VMEM((16 * K16_RES_ROWS, 2 * LANES), BF16),
            pltpu.VMEM((4, SPAN // 4, LANES), F32),
            pltpu.VMEM((4, SPAN // 4, LANES), F32),
            pltpu.VMEM((4, SPAN // 4, LANES), F32),
            pltpu.VMEM((len(DILATIONS), SPAN, LANES), F32),
            pltpu.VMEM((len(DILATIONS), SPAN, LANES), F32),
        ],
        compiler_params=pltpu.CompilerParams(
            dimension_semantics=("arbitrary", "arbitrary", "arbitrary"),
            vmem_limit_bytes=VMEM_LIMIT),
        name="attn",
    )(slopes, q, k, v)


def _tail_kernel(x_ref, attn_ref, sgu_ref, p_ref, ga_ref, wout_ref, gpm_ref, gpf_ref,
                 wgu_ref, wd_ref, gpo_ref, wpg_ref, bpg_ref, wpp_ref, o_ref):
    d_ff = wd_ref.shape[0]
    sub = x_ref.shape[0] // TAIL_CHAINS
    for c in range(TAIL_CHAINS):
        rows = slice(c * sub, (c + 1) * sub)
        attn = jnp.concatenate([attn_ref[s, rows, :] for s in range(attn_ref.shape[0])], axis=-1)
        an = _rms(attn, ga_ref[...]).astype(BF16)
        groups = jnp.concatenate([an, sgu_ref[rows, :]], axis=-1)
        mixed = jnp.dot(groups, wout_ref[...], preferred_element_type=F32)
        h = x_ref[rows, :] + _rms(mixed, gpm_ref[...])
        f = _rms(h, gpf_ref[...]).astype(BF16)
        gu = jnp.dot(f, wgu_ref[...], preferred_element_type=F32)
        hid = (jax.nn.silu(gu[:, :d_ff]) * gu[:, d_ff:]).astype(BF16)
        y = jnp.dot(hid, wd_ref[...], preferred_element_type=F32)
        h = h + _rms(y, gpo_ref[...])
        gate = jax.nn.sigmoid(jnp.dot(h.astype(BF16), wpg_ref[...], preferred_element_type=F32)
                              + bpg_ref[...])
        pe = jnp.dot(p_ref[rows, :].astype(BF16), wpp_ref[...], preferred_element_type=F32)
        o_ref[rows, :] = h + gate * pe


def _tail(x2, attn, sgu, p2, *params):
    n, d = x2.shape
    tm = TAIL_TILE

    def tile(a):
        return pl.BlockSpec((tm, a.shape[1]), lambda i: (i, 0))

    return pl.pallas_call(
        _tail_kernel,
        grid=(n // tm,),
        in_specs=[tile(x2), pl.BlockSpec((attn.shape[0], tm, LANES), lambda i: (0, i, 0)),
                  tile(sgu), tile(p2)] + [_const_spec(a.shape) for a in params],
        out_specs=pl.BlockSpec((tm, d), lambda i: (i, 0)),
        out_shape=jax.ShapeDtypeStruct((n, d), F32),
        compiler_params=pltpu.CompilerParams(dimension_semantics=("arbitrary",),
                                             vmem_limit_bytes=VMEM_LIMIT),
        name="tail",
    )(x2, attn, sgu, p2, *params)


def _group_major(w_in, attn_w, sgu_w):
    gd = sgu_w // N_SGU_GROUPS
    u0 = 3 * attn_w
    z0 = u0 + sgu_w
    cols = []
    for g in range(N_SGU_GROUPS):
        cols += [w_in[:, u0 + g * gd:u0 + (g + 1) * gd], w_in[:, z0 + g * gd:z0 + (g + 1) * gd]]
    return jnp.concatenate(cols + [w_in[:, :u0]], axis=1)


def kernel(x, p, ln_pre_mix, w_in, sgu_ln_g, sgu_ln_b, w_spatial, b_spatial, attn_out_norm,
           sgu_out_norm, w_out, ln_post_mix, ln_pre_ffn, w_gate_up, w_down, ln_post_ffn,
           w_pe_gate, b_pe_gate, w_pe_proj):
    batch, seq, d = x.shape
    depth = w_in.shape[0]
    attn_w = attn_out_norm.shape[1]
    sgu_w = sgu_out_norm.shape[1]
    n_heads = attn_w // HEAD_DIM
    slopes = 2.0 ** (-8.0 * (jnp.arange(n_heads, dtype=F32) + 1.0) / n_heads)

    def row(a):
        return a.reshape(1, -1)

    h = x.reshape(batch * seq, d)
    for i in range(depth):
        q, k, v, sgu = _proj_sgu(
            h, row(ln_pre_mix[i]), _group_major(w_in[i], attn_w, sgu_w).astype(BF16),
            row(sgu_ln_g[i]), row(sgu_ln_b[i]),
            w_spatial[i], b_spatial[i][:, :, None], row(sgu_out_norm[i]), attn_w, sgu_w)
        attn = _attention(q, k, v, slopes, batch, seq)
        h = _tail(h, attn, sgu, p[i].reshape(batch * seq, -1),
                  row(attn_out_norm[i]), w_out[i].astype(BF16), row(ln_post_mix[i]),
                  row(ln_pre_ffn[i]), w_gate_up[i].astype(BF16), w_down[i].astype(BF16),
                  row(ln_post_ffn[i]), w_pe_gate[i].astype(BF16), row(b_pe_gate[i]),
                  w_pe_proj[i].astype(BF16))
    return h.reshape(batch, seq, d)
```

```python
import functools

import jax
import jax.numpy as jnp
from jax import lax
from jax.experimental import pallas as pl
from jax.experimental.pallas import tpu as pltpu

F32 = jnp.float32
BF16 = jnp.bfloat16

HEAD_DIM = 64
QBLK = 128
WIN = 2 * QBLK
DILATIONS = (1, 4, 16)
SPAN = QBLK * DILATIONS[-1]
SGU_CHUNK = 128
N_SGU_GROUPS = 4
EPS = 1e-6
NEG = -1e30

LANES = 128
HEADS_PER_SLAB = LANES // HEAD_DIM
BLOCK_UNROLL = 8

PROJ_TILE = 512
TAIL_TILE = 512
TAIL_CHAINS = 1
VMEM_LIMIT = 56 * 1024 * 1024


def _rms(x, g):
    return x * lax.rsqrt(jnp.mean(x * x, axis=-1, keepdims=True) + EPS) * g


def _const_spec(shape):
    zeros = (0,) * len(shape)
    return pl.BlockSpec(shape, lambda *_: zeros, pipeline_mode=pl.Buffered(1))


def _proj_sgu_kernel(x_ref, g_ref, w_ref, lng_ref, lnb_ref, ws_ref, bs_ref, gout_ref,
                     q_ref, k_ref, v_ref, sgu_ref, *, attn_w, sgu_w):
    tm = x_ref.shape[0]
    gd = sgu_w // N_SGU_GROUPS
    a = _rms(x_ref[...], g_ref[...]).astype(BF16)

    def project(c0, width):
        return jnp.dot(a, w_ref[:, c0:c0 + width], preferred_element_type=F32)

    def emit_slabs(ref, cols):
        for s in range(attn_w // LANES):
            ref[s] = cols[:, s * LANES:(s + 1) * LANES]

    row = lax.broadcasted_iota(jnp.int32, (SGU_CHUNK, SGU_CHUNK), 0)
    col = lax.broadcasted_iota(jnp.int32, (SGU_CHUNK, SGU_CHUNK), 1)
    causal = row >= col
    outs = []
    ssq = jnp.zeros((tm, 1), F32)
    uz_pairs = [project(0, 4 * gd)]
    for g in range(N_SGU_GROUPS):
        uz = uz_pairs[g // 2][:, (g % 2) * 2 * gd:(g % 2 + 1) * 2 * gd]
        if g == 0:
            uz_pairs.append(project(4 * gd, 4 * gd))
        elif g == 1:
            emit_slabs(q_ref, project(2 * sgu_w, attn_w) * (HEAD_DIM ** -0.5))
        elif g == 2:
            emit_slabs(k_ref, project(2 * sgu_w + attn_w, attn_w))
        elif g == 3:
            emit_slabs(v_ref, project(2 * sgu_w + 2 * attn_w, attn_w))
        u = jax.nn.gelu(uz[:, :gd])
        z = jax.nn.gelu(uz[:, gd:])
        zc = z - jnp.mean(z, axis=-1, keepdims=True)
        zn = zc * lax.rsqrt(jnp.mean(zc * zc, axis=-1, keepdims=True) + EPS)
        zn = (zn * lng_ref[...] + lnb_ref[...]).astype(BF16)
        wm = jnp.where(causal, ws_ref[g], 0.0).astype(BF16)
        mixed = [jnp.dot(wm, zn[c * SGU_CHUNK:(c + 1) * SGU_CHUNK], preferred_element_type=F32)
                 + bs_ref[g] for c in range(tm // SGU_CHUNK)]
        o = u * jnp.concatenate(mixed, axis=0)
        ssq = ssq + jnp.sum(o * o, axis=-1, keepdims=True)
        outs.append(o)
    scale = lax.rsqrt(ssq / sgu_w + EPS)
    for g in range(N_SGU_GROUPS):
        sgu_ref[:, g * gd:(g + 1) * gd] = (outs[g] * scale * gout_ref[:, g * gd:(g + 1) * gd]).astype(BF16)


def _proj_sgu(x2, g_pre, w_in, ln_g, ln_b, w_sp, b_sp, g_sgu, attn_w, sgu_w):
    n, d = x2.shape
    tm = PROJ_TILE
    n_slabs = attn_w // LANES
    slab = jax.ShapeDtypeStruct((n_slabs, n, LANES), F32)
    slab_spec = pl.BlockSpec((n_slabs, tm, LANES), lambda i: (0, i, 0))
    return pl.pallas_call(
        functools.partial(_proj_sgu_kernel, attn_w=attn_w, sgu_w=sgu_w),
        grid=(n // tm,),
        in_specs=[
            pl.BlockSpec((tm, d), lambda i: (i, 0)),
            _const_spec(g_pre.shape), _const_spec(w_in.shape), _const_spec(ln_g.shape),
            _const_spec(ln_b.shape), _const_spec(w_sp.shape), _const_spec(b_sp.shape),
            _const_spec(g_sgu.shape),
        ],
        out_specs=[slab_spec, slab_spec, slab_spec, pl.BlockSpec((tm, sgu_w), lambda i: (i, 0))],
        out_shape=[slab, slab, slab, jax.ShapeDtypeStruct((n, sgu_w), BF16)],
        compiler_params=pltpu.CompilerParams(dimension_semantics=("arbitrary",),
                                             vmem_limit_bytes=VMEM_LIMIT),
        name="proj_sgu",
    )(x2, g_pre, w_in, ln_g, ln_b, w_sp, b_sp, g_sgu)


K1_ROWS = QBLK + SPAN
K4_RES_ROWS = QBLK + SPAN // 4
K16_RES_ROWS = WIN


def _attn_kernel(slopes_ref, q_ref, k_ref, v_ref, o_ref,
                 bias_ref, q4, q16, k1, v1, k4, v4, k16, v16, aq, ak, av, ob, lb):
    slab = pl.program_id(1)
    span = pl.program_id(2)
    lane = lax.broadcasted_iota(jnp.int32, (QBLK, LANES), 1)
    head0 = lane < HEAD_DIM

    @pl.when(span == 0)
    def _start_of_sequence():
        zeros = jnp.zeros((QBLK, LANES), BF16)
        k1[0:QBLK] = zeros
        v1[0:QBLK, 0:LANES] = zeros
        for r in range(4):
            k4[r * K4_RES_ROWS: r * K4_RES_ROWS + QBLK] = zeros
            v4[r * K4_RES_ROWS: r * K4_RES_ROWS + QBLK, 0:LANES] = zeros
        for r in range(16):
            k16[r * K16_RES_ROWS: r * K16_RES_ROWS + QBLK] = zeros
            v16[r * K16_RES_ROWS: r * K16_RES_ROWS + QBLK, 0:LANES] = zeros
        for ref in (v1, v4, v16):
            ref[:, LANES:2 * LANES] = jnp.ones((ref.shape[0], LANES), BF16)
        i = lax.broadcasted_iota(jnp.int32, (WIN, WIN), 0)
        j = lax.broadcasted_iota(jnp.int32, (WIN, WIN), 1)
        steps = QBLK + (i & (QBLK - 1)) - j
        valid = (steps >= 0) & (steps <= QBLK)
        slope = jnp.where(i < QBLK, slopes_ref[HEADS_PER_SLAB * slab],
                          slopes_ref[HEADS_PER_SLAB * slab + 1])
        for b, d in enumerate(DILATIONS):
            bias = jnp.where(valid, -slope * (steps * d).astype(F32), NEG)
            bias_ref[b, 1] = bias
            bias_ref[b, 0] = jnp.where(j >= QBLK, bias, NEG)

    k1[QBLK:K1_ROWS] = k_ref[0].astype(BF16)
    v1[QBLK:K1_ROWS, 0:LANES] = v_ref[0].astype(BF16)
    for r in range(4):
        for src, tmp in ((q_ref, aq), (k_ref, ak), (v_ref, av)):
            tmp[r] = src[0, pl.ds(r, SPAN // 4, stride=4), :]
        q4[r * (SPAN // 4):(r + 1) * (SPAN // 4)] = aq[r].astype(BF16)
        k4[r * K4_RES_ROWS + QBLK:(r + 1) * K4_RES_ROWS] = ak[r].astype(BF16)
        v4[r * K4_RES_ROWS + QBLK:(r + 1) * K4_RES_ROWS, 0:LANES] = av[r].astype(BF16)
        for r2 in range(4):
            r16 = 4 * r2 + r
            q16[r16 * QBLK:(r16 + 1) * QBLK] = aq[r, pl.ds(r2, QBLK, stride=4), :].astype(BF16)
            k16[r16 * K16_RES_ROWS + QBLK:(r16 + 1) * K16_RES_ROWS] = (
                ak[r, pl.ds(r2, QBLK, stride=4), :].astype(BF16))
            v16[r16 * K16_RES_ROWS + QBLK:(r16 + 1) * K16_RES_ROWS, 0:LANES] = (
                av[r, pl.ds(r2, QBLK, stride=4), :].astype(BF16))

    not_first = jnp.where(span == 0, 0, 1)

    def block(b, q, ks, vs, koff, sel):
        zero = jnp.zeros_like(q)
        qq = jnp.concatenate([jnp.where(head0, q, zero), jnp.where(head0, zero, q)], axis=0)
        kw = ks[pl.ds(pl.multiple_of(koff, QBLK), WIN), :]
        vw = vs[pl.ds(pl.multiple_of(koff, QBLK), WIN), :]
        s = lax.dot_general(qq, kw, (((1,), (1,)), ((), ())), preferred_element_type=F32)
        s = s + bias_ref[b, sel]
        m = jnp.max(s, axis=1, keepdims=True)
        p = jnp.exp(s - m).astype(BF16)
        acc = jnp.dot(p, vw, preferred_element_type=F32)
        o_t = jnp.where(head0, acc[0:QBLK, 0:LANES], acc[QBLK:WIN, 0:LANES])
        l_t = jnp.where(head0, acc[0:QBLK, LANES:], acc[QBLK:WIN, LANES:])
        m_t = jnp.where(head0, m[0:QBLK], m[QBLK:WIN])
        return o_t / l_t, m_t + jnp.log(l_t)

    def branch1(nb, c):
        sel = jnp.where(nb == 0, not_first, 1)
        start = pl.multiple_of(nb * QBLK, QBLK)
        q = q_ref[0, pl.ds(start, QBLK), :].astype(BF16)
        o_t, lse_t = block(0, q, k1, v1, start, sel)
        ob[0, pl.ds(start, QBLK), :] = o_t
        lb[0, pl.ds(start, QBLK), :] = lse_t
        return c

    def branch4(it, c):
        r = it // 4
        nb = it % 4
        sel = jnp.where(nb == 0, not_first, 1)
        q = q4[pl.ds(pl.multiple_of(it * QBLK, QBLK), QBLK), :]
        o_t, lse_t = block(1, q, k4, v4, r * K4_RES_ROWS + nb * QBLK, sel)
        start = pl.multiple_of(it * QBLK, QBLK)
        ob[1, pl.ds(start, QBLK), :] = o_t
        lb[1, pl.ds(start, QBLK), :] = lse_t
        return c

    def branch16(r, c):
        q = q16[pl.ds(pl.multiple_of(r * QBLK, QBLK), QBLK), :]
        o_t, lse_t = block(2, q, k16, v16, r * K16_RES_ROWS, not_first)
        start = (r % 4) * (SPAN // 4) + r // 4
        ob[2, pl.ds(start, QBLK, stride=4), :] = o_t
        lb[2, pl.ds(start, QBLK, stride=4), :] = lse_t
        return c

    lax.fori_loop(0, SPAN // QBLK, branch1, 0, unroll=BLOCK_UNROLL)
    lax.fori_loop(0, 16, branch4, 0, unroll=BLOCK_UNROLL)
    lax.fori_loop(0, 16, branch16, 0, unroll=BLOCK_UNROLL)

    def merge(c, carry):
        rows = pl.ds(pl.multiple_of(c * WIN, WIN), WIN)
        tokens = pl.ds(c // 2 + (c % 2) * (4 * WIN), WIN, stride=4)
        l0, l1, l2 = lb[0, tokens, :], lb[1, rows, :], lb[2, rows, :]
        mx = jnp.maximum(jnp.maximum(l0, l1), l2)
        w0, w1, w2 = jnp.exp(l0 - mx), jnp.exp(l1 - mx), jnp.exp(l2 - mx)
        num = w0 * ob[0, tokens, :] + w1 * ob[1, rows, :] + w2 * ob[2, rows, :]
        o_ref[0, tokens, :] = num / (w0 + w1 + w2)
        return carry

    lax.fori_loop(0, SPAN // WIN, merge, 0)

    for ref, n_res, res_rows in ((k1, 1, K1_ROWS), (v1, 1, K1_ROWS), (k4, 4, K4_RES_ROWS),
                                 (v4, 4, K4_RES_ROWS), (k16, 16, K16_RES_ROWS), (v16, 16, K16_RES_ROWS)):
        for r in range(n_res):
            ref[r * res_rows: r * res_rows + QBLK, 0:LANES] = (
                ref[(r + 1) * res_rows - QBLK:(r + 1) * res_rows, 0:LANES])


def _attention(q, k, v, slopes, batch, seq):
    n_slabs, n, _ = q.shape
    assert seq % SPAN == 0
    spans = seq // SPAN
    in_spec = pl.BlockSpec((1, SPAN, LANES), lambda b, s, t: (s, b * spans + t, 0))
    return pl.pallas_call(
        _attn_kernel,
        grid=(batch, n_slabs, spans),
        in_specs=[pl.BlockSpec(memory_space=pltpu.SMEM), in_spec, in_spec, in_spec],
        out_specs=in_spec,
        out_shape=jax.ShapeDtypeStruct((n_slabs, n, LANES), F32),
        scratch_shapes=[
            pltpu.VMEM((len(DILATIONS), 2, WIN, WIN), F32),
            pltpu.VMEM((SPAN, LANES), BF16),
            pltpu.VMEM((SPAN, LANES), BF16),
            pltpu.VMEM((K1_ROWS, LANES), BF16),
            pltpu.VMEM((K1_ROWS, 2 * LANES), BF16),
            pltpu.VMEM((4 * K4_RES_ROWS, LANES), BF16),
            pltpu.VMEM((4 * K4_RES_ROWS, 2 * LANES), BF16),
            pltpu.VMEM((16 * K16_RES_ROWS, LANES), BF16),
            pltpu.VMEM((16 * K16_RES_ROWS, 2 * LANES), BF16),
            pltpu.VMEM((4, SPAN // 4, LANES), F32),
            pltpu.VMEM((4, SPAN // 4, LANES), F32),
            pltpu.VMEM((4, SPAN // 4, LANES), F32),
            pltpu.VMEM((len(DILATIONS), SPAN, LANES), F32),
            pltpu.VMEM((len(DILATIONS), SPAN, LANES), F32),
        ],
        compiler_params=pltpu.CompilerParams(
            dimension_semantics=("arbitrary", "arbitrary", "arbitrary"),
            vmem_limit_bytes=VMEM_LIMIT),
        name="attn",
    )(slopes, q, k, v)


def _tail_kernel(x_ref, attn_ref, sgu_ref, p_ref, ga_ref, wout_ref, gpm_ref, gpf_ref,
                 wgu_ref, wd_ref, gpo_ref, wpg_ref, bpg_ref, wpp_ref, o_ref):
    d_ff = wd_ref.shape[0]
    sub = x_ref.shape[0] // TAIL_CHAINS
    for c in range(TAIL_CHAINS):
        rows = slice(c * sub, (c + 1) * sub)
        attn = jnp.concatenate([attn_ref[s, rows, :] for s in range(attn_ref.shape[0])], axis=-1)
        an = _rms(attn, ga_ref[...]).astype(BF16)
        groups = jnp.concatenate([an, sgu_ref[rows, :]], axis=-1)
        mixed = jnp.dot(groups, wout_ref[...], preferred_element_type=F32)
        h = x_ref[rows, :] + _rms(mixed, gpm_ref[...])
        f = _rms(h, gpf_ref[...]).astype(BF16)
        gu = jnp.dot(f, wgu_ref[...], preferred_element_type=F32)
        hid = (jax.nn.silu(gu[:, :d_ff]) * gu[:, d_ff:]).astype(BF16)
        y = jnp.dot(hid, wd_ref[...], preferred_element_type=F32)
        h = h + _rms(y, gpo_ref[...])
        gate = jax.nn.sigmoid(jnp.dot(h.astype(BF16), wpg_ref[...], preferred_element_type=F32)
                              + bpg_ref[...])
        pe = jnp.dot(p_ref[rows, :].astype(BF16), wpp_ref[...], preferred_element_type=F32)
        o_ref[rows, :] = h + gate * pe


def _tail(x2, attn, sgu, p2, *params):
    n, d = x2.shape
    tm = TAIL_TILE

    def tile(a):
        return pl.BlockSpec((tm, a.shape[1]), lambda i: (i, 0))

    return pl.pallas_call(
        _tail_kernel,
        grid=(n // tm,),
        in_specs=[tile(x2), pl.BlockSpec((attn.shape[0], tm, LANES), lambda i: (0, i, 0)),
                  tile(sgu), tile(p2)] + [_const_spec(a.shape) for a in params],
        out_specs=pl.BlockSpec((tm, d), lambda i: (i, 0)),
        out_shape=jax.ShapeDtypeStruct((n, d), F32),
        compiler_params=pltpu.CompilerParams(dimension_semantics=("arbitrary",),
                                             vmem_limit_bytes=VMEM_LIMIT),
        name="tail",
    )(x2, attn, sgu, p2, *params)


def _group_major(w_in, attn_w, sgu_w):
    gd = sgu_w // N_SGU_GROUPS
    u0 = 3 * attn_w
    z0 = u0 + sgu_w
    cols = []
    for g in range(N_SGU_GROUPS):
        cols += [w_in[:, u0 + g * gd:u0 + (g + 1) * gd], w_in[:, z0 + g * gd:z0 + (g + 1) * gd]]
    return jnp.concatenate(cols + [w_in[:, :u0]], axis=1)


def kernel(x, p, ln_pre_mix, w_in, sgu_ln_g, sgu_ln_b, w_spatial, b_spatial, attn_out_norm,
           sgu_out_norm, w_out, ln_post_mix, ln_pre_ffn, w_gate_up, w_down, ln_post_ffn,
           w_pe_gate, b_pe_gate, w_pe_proj):
    batch, seq, d = x.shape
    depth = w_in.shape[0]
    attn_w = attn_out_norm.shape[1]
    sgu_w = sgu_out_norm.shape[1]
    n_heads = attn_w // HEAD_DIM
    slopes = 2.0 ** (-8.0 * (jnp.arange(n_heads, dtype=F32) + 1.0) / n_heads)

    def row(a):
        return a.reshape(1, -1)

    h = x.reshape(batch * seq, d)
    for i in range(depth):
        q, k, v, sgu = _proj_sgu(
            h, row(ln_pre_mix[i]), _group_major(w_in[i], attn_w, sgu_w).astype(BF16),
            row(sgu_ln_g[i]), row(sgu_ln_b[i]),
            w_spatial[i], b_spatial[i][:, :, None], row(sgu_out_norm[i]), attn_w, sgu_w)
        attn = _attention(q, k, v, slopes, batch, seq)
        h = _tail(h, attn, sgu, p[i].reshape(batch * seq, -1),
                  row(attn_out_norm[i]), w_out[i].astype(BF16), row(ln_post_mix[i]),
                  row(ln_pre_ffn[i]), w_gate_up[i].astype(BF16), w_down[i].astype(BF16),
                  row(ln_post_ffn[i]), w_pe_gate[i].astype(BF16), row(b_pe_gate[i]),
                  w_pe_proj[i].astype(BF16))
    return h.reshape(batch, seq, d)
```

```python
import functools

import jax
import jax.numpy as jnp
from jax import lax
from jax.experimental import pallas as pl
from jax.experimental.pallas import tpu as pltpu

F32 = jnp.float32
BF16 = jnp.bfloat16

HEAD_DIM = 64
QBLK = 128
WIN = 2 * QBLK
DILATIONS = (1, 4, 16)
SPAN = QBLK * DILATIONS[-1]
SGU_CHUNK = 128
N_SGU_GROUPS = 4
EPS = 1e-6
NEG = -1e30

LANES = 128
HEADS_PER_SLAB = LANES // HEAD_DIM
BLOCK_UNROLL = 16

PROJ_TILE = 512
TAIL_TILE = 512
TAIL_CHAINS = 2
VMEM_LIMIT = 56 * 1024 * 1024


def _rms(x, g):
    return x * lax.rsqrt(jnp.mean(x * x, axis=-1, keepdims=True) + EPS) * g


def _const_spec(shape):
    zeros = (0,) * len(shape)
    return pl.BlockSpec(shape, lambda *_: zeros, pipeline_mode=pl.Buffered(1))


def _proj_sgu_kernel(x_ref, g_ref, w_ref, lng_ref, lnb_ref, ws_ref, bs_ref, gout_ref,
                     q_ref, k_ref, v_ref, sgu_ref, *, attn_w, sgu_w):
    tm = x_ref.shape[0]
    gd = sgu_w // N_SGU_GROUPS
    a = _rms(x_ref[...], g_ref[...]).astype(BF16)

    def project(c0, width):
        return jnp.dot(a, w_ref[:, c0:c0 + width], preferred_element_type=F32)

    def emit_slabs(ref, cols):
        for s in range(attn_w // LANES):
            ref[s] = cols[:, s * LANES:(s + 1) * LANES]

    row = lax.broadcasted_iota(jnp.int32, (SGU_CHUNK, SGU_CHUNK), 0)
    col = lax.broadcasted_iota(jnp.int32, (SGU_CHUNK, SGU_CHUNK), 1)
    causal = row >= col
    outs = []
    ssq = jnp.zeros((tm, 1), F32)
    uz_pairs = [project(0, 4 * gd)]
    for g in range(N_SGU_GROUPS):
        uz = uz_pairs[g // 2][:, (g % 2) * 2 * gd:(g % 2 + 1) * 2 * gd]
        if g == 0:
            uz_pairs.append(project(4 * gd, 4 * gd))
        elif g == 1:
            emit_slabs(q_ref, project(2 * sgu_w, attn_w) * (HEAD_DIM ** -0.5))
        elif g == 2:
            emit_slabs(k_ref, project(2 * sgu_w + attn_w, attn_w))
        elif g == 3:
            emit_slabs(v_ref, project(2 * sgu_w + 2 * attn_w, attn_w))
        u = jax.nn.gelu(uz[:, :gd])
        z = jax.nn.gelu(uz[:, gd:])
        zc = z - jnp.mean(z, axis=-1, keepdims=True)
        zn = zc * lax.rsqrt(jnp.mean(zc * zc, axis=-1, keepdims=True) + EPS)
        zn = (zn * lng_ref[...] + lnb_ref[...]).astype(BF16)
        wm = jnp.where(causal, ws_ref[g], 0.0).astype(BF16)
        mixed = [jnp.dot(wm, zn[c * SGU_CHUNK:(c + 1) * SGU_CHUNK], preferred_element_type=F32)
                 + bs_ref[g] for c in range(tm // SGU_CHUNK)]
        o = u * jnp.concatenate(mixed, axis=0)
        ssq = ssq + jnp.sum(o * o, axis=-1, keepdims=True)
        outs.append(o)
    scale = lax.rsqrt(ssq / sgu_w + EPS)
    for g in range(N_SGU_GROUPS):
        sgu_ref[:, g * gd:(g + 1) * gd] = (outs[g] * scale * gout_ref[:, g * gd:(g + 1) * gd]).astype(BF16)


def _proj_sgu(x2, g_pre, w_in, ln_g, ln_b, w_sp, b_sp, g_sgu, attn_w, sgu_w):
    n, d = x2.shape
    tm = PROJ_TILE
    n_slabs = attn_w // LANES
    slab = jax.ShapeDtypeStruct((n_slabs, n, LANES), F32)
    slab_spec = pl.BlockSpec((n_slabs, tm, LANES), lambda i: (0, i, 0))
    return pl.pallas_call(
        functools.partial(_proj_sgu_kernel, attn_w=attn_w, sgu_w=sgu_w),
        grid=(n // tm,),
        in_specs=[
            pl.BlockSpec((tm, d), lambda i: (i, 0)),
            _const_spec(g_pre.shape), _const_spec(w_in.shape), _const_spec(ln_g.shape),
            _const_spec(ln_b.shape), _const_spec(w_sp.shape), _const_spec(b_sp.shape),
            _const_spec(g_sgu.shape),
        ],
        out_specs=[slab_spec, slab_spec, slab_spec, pl.BlockSpec((tm, sgu_w), lambda i: (i, 0))],
        out_shape=[slab, slab, slab, jax.ShapeDtypeStruct((n, sgu_w), BF16)],
        compiler_params=pltpu.CompilerParams(dimension_semantics=("arbitrary",),
                                             vmem_limit_bytes=VMEM_LIMIT),
        name="proj_sgu",
    )(x2, g_pre, w_in, ln_g, ln_b, w_sp, b_sp, g_sgu)


K1_ROWS = QBLK + SPAN
K4_RES_ROWS = QBLK + SPAN // 4
K16_RES_ROWS = WIN


def _attn_kernel(slopes_ref, q_ref, k_ref, v_ref, o_ref,
                 bias_ref, q4, q16, k1, v1, k4, v4, k16, v16, aq, ak, av, ob, lb):
    slab = pl.program_id(1)
    span = pl.program_id(2)
    lane = lax.broadcasted_iota(jnp.int32, (QBLK, LANES), 1)
    head0 = lane < HEAD_DIM

    @pl.when(span == 0)
    def _start_of_sequence():
        zeros = jnp.zeros((QBLK, LANES), BF16)
        k1[0:QBLK] = zeros
        v1[0:QBLK, 0:LANES] = zeros
        for r in range(4):
            k4[r * K4_RES_ROWS: r * K4_RES_ROWS + QBLK] = zeros
            v4[r * K4_RES_ROWS: r * K4_RES_ROWS + QBLK, 0:LANES] = zeros
        for r in range(16):
            k16[r * K16_RES_ROWS: r * K16_RES_ROWS + QBLK] = zeros
            v16[r * K16_RES_ROWS: r * K16_RES_ROWS + QBLK, 0:LANES] = zeros
        for ref in (v1, v4, v16):
            ref[:, LANES:2 * LANES] = jnp.ones((ref.shape[0], LANES), BF16)
        i = lax.broadcasted_iota(jnp.int32, (WIN, WIN), 0)
        j = lax.broadcasted_iota(jnp.int32, (WIN, WIN), 1)
        steps = QBLK + (i & (QBLK - 1)) - j
        valid = (steps >= 0) & (steps <= QBLK)
        slope = jnp.where(i < QBLK, slopes_ref[HEADS_PER_SLAB * slab],
                          slopes_ref[HEADS_PER_SLAB * slab + 1])
        for b, d in enumerate(DILATIONS):
            bias = jnp.where(valid, -slope * (steps * d).astype(F32), NEG)
            bias_ref[b, 1] = bias
            bias_ref[b, 0] = jnp.where(j >= QBLK, bias, NEG)

    k1[QBLK:K1_ROWS] = k_ref[0].astype(BF16)
    v1[QBLK:K1_ROWS, 0:LANES] = v_ref[0].astype(BF16)
    for r in range(4):
        for src, tmp in ((q_ref, aq), (k_ref, ak), (v_ref, av)):
            tmp[r] = src[0, pl.ds(r, SPAN // 4, stride=4), :]
        q4[r * (SPAN // 4):(r + 1) * (SPAN // 4)] = aq[r].astype(BF16)
        k4[r * K4_RES_ROWS + QBLK:(r + 1) * K4_RES_ROWS] = ak[r].astype(BF16)
        v4[r * K4_RES_ROWS + QBLK:(r + 1) * K4_RES_ROWS, 0:LANES] = av[r].astype(BF16)
        for r2 in range(4):
            r16 = 4 * r2 + r
            q16[r16 * QBLK:(r16 + 1) * QBLK] = aq[r, pl.ds(r2, QBLK, stride=4), :].astype(BF16)
            k16[r16 * K16_RES_ROWS + QBLK:(r16 + 1) * K16_RES_ROWS] = (
                ak[r, pl.ds(r2, QBLK, stride=4), :].astype(BF16))
            v16[r16 * K16_RES_ROWS + QBLK:(r16 + 1) * K16_RES_ROWS, 0:LANES] = (
                av[r, pl.ds(r2, QBLK, stride=4), :].astype(BF16))

    not_first = jnp.where(span == 0, 0, 1)

    def block(b, q, ks, vs, koff, sel):
        zero = jnp.zeros_like(q)
        qq = jnp.concatenate([jnp.where(head0, q, zero), jnp.where(head0, zero, q)], axis=0)
        kw = ks[pl.ds(pl.multiple_of(koff, QBLK), WIN), :]
        vw = vs[pl.ds(pl.multiple_of(koff, QBLK), WIN), :]
        s = lax.dot_general(qq, kw, (((1,), (1,)), ((), ())), preferred_element_type=F32)
        s = s + bias_ref[b, sel]
        m = jnp.max(s, axis=1, keepdims=True)
        p = jnp.exp(s - m).astype(BF16)
        acc = jnp.dot(p, vw, preferred_element_type=F32)
        o_t = jnp.where(head0, acc[0:QBLK, 0:LANES], acc[QBLK:WIN, 0:LANES])
        l_t = jnp.where(head0, acc[0:QBLK, LANES:], acc[QBLK:WIN, LANES:])
        m_t = jnp.where(head0, m[0:QBLK], m[QBLK:WIN])
        return o_t / l_t, m_t + jnp.log(l_t)

    def branch1(nb, c):
        sel = jnp.where(nb == 0, not_first, 1)
        start = pl.multiple_of(nb * QBLK, QBLK)
        q = q_ref[0, pl.ds(start, QBLK), :].astype(BF16)
        o_t, lse_t = block(0, q, k1, v1, start, sel)
        ob[0, pl.ds(start, QBLK), :] = o_t
        lb[0, pl.ds(start, QBLK), :] = lse_t
        return c

    def branch4(it, c):
        r = it // 4
        nb = it % 4
        sel = jnp.where(nb == 0, not_first, 1)
        q = q4[pl.ds(pl.multiple_of(it * QBLK, QBLK), QBLK), :]
        o_t, lse_t = block(1, q, k4, v4, r * K4_RES_ROWS + nb * QBLK, sel)
        start = pl.multiple_of(it * QBLK, QBLK)
        ob[1, pl.ds(start, QBLK), :] = o_t
        lb[1, pl.ds(start, QBLK), :] = lse_t
        return c

    def branch16(r, c):
        q = q16[pl.ds(pl.multiple_of(r * QBLK, QBLK), QBLK), :]
        o_t, lse_t = block(2, q, k16, v16, r * K16_RES_ROWS, not_first)
        start = (r % 4) * (SPAN // 4) + r // 4
        ob[2, pl.ds(start, QBLK, stride=4), :] = o_t
        lb[2, pl.ds(start, QBLK, stride=4), :] = lse_t
        return c

    lax.fori_loop(0, SPAN // QBLK, branch1, 0, unroll=BLOCK_UNROLL)
    lax.fori_loop(0, 16, branch4, 0, unroll=BLOCK_UNROLL)
    lax.fori_loop(0, 16, branch16, 0, unroll=BLOCK_UNROLL)

    def merge(c, carry):
        rows = pl.ds(pl.multiple_of(c * WIN, WIN), WIN)
        tokens = pl.ds(c // 2 + (c % 2) * (4 * WIN), WIN, stride=4)
        l0, l1, l2 = lb[0, tokens, :], lb[1, rows, :], lb[2, rows, :]
        mx = jnp.maximum(jnp.maximum(l0, l1), l2)
        w0, w1, w2 = jnp.exp(l0 - mx), jnp.exp(l1 - mx), jnp.exp(l2 - mx)
        num = w0 * ob[0, tokens, :] + w1 * ob[1, rows, :] + w2 * ob[2, rows, :]
        o_ref[0, tokens, :] = num / (w0 + w1 + w2)
        return carry

    lax.fori_loop(0, SPAN // WIN, merge, 0)

    for ref, n_res, res_rows in ((k1, 1, K1_ROWS), (v1, 1, K1_ROWS), (k4, 4, K4_RES_ROWS),
                                 (v4, 4, K4_RES_ROWS), (k16, 16, K16_RES_ROWS), (v16, 16, K16_RES_ROWS)):
        for r in range(n_res):
            ref[r * res_rows: r * res_rows + QBLK, 0:LANES] = (
                ref[(r + 1) * res_rows - QBLK:(r + 1) * res_rows, 0:LANES])


def _attention(q, k, v, slopes, batch, seq):
    n_slabs, n, _ = q.shape
    assert seq % SPAN == 0
    spans = seq // SPAN
    in_spec = pl.BlockSpec((1, SPAN, LANES), lambda b, s, t: (s, b * spans + t, 0))
    return pl.pallas_call(
        _attn_kernel,
        grid=(batch, n_slabs, spans),
        in_specs=[pl.BlockSpec(memory_space=pltpu.SMEM), in_spec, in_spec, in_spec],
        out_specs=in_spec,
        out_shape=jax.ShapeDtypeStruct((n_slabs, n, LANES), F32),
        scratch_shapes=[
            pltpu.VMEM((len(DILATIONS), 2, WIN, WIN), F32),
            pltpu.VMEM((SPAN, LANES), BF16),
            pltpu.VMEM((SPAN, LANES), BF16),
            pltpu.VMEM((K1_ROWS, LANES), BF16),
            pltpu.VMEM((K1_ROWS, 2 * LANES), BF16),
            pltpu.VMEM((4 * K4_RES_ROWS, LANES), BF16),
            pltpu.VMEM((4 * K4_RES_ROWS, 2 * LANES), BF16),
            pltpu.VMEM((16 * K16_RES_ROWS, LANES), BF16),
            pltpu.VMEM((16 * K16_RES_ROWS, 2 * LANES), BF16),
            pltpu.VMEM((4, SPAN // 4, LANES), F32),
            pltpu.VMEM((4, SPAN // 4, LANES), F32),
            pltpu.VMEM((4, SPAN // 4, LANES), F32),
            pltpu.VMEM((len(DILATIONS), SPAN, LANES), F32),
            pltpu.VMEM((len(DILATIONS), SPAN, LANES), F32),
        ],
        compiler_params=pltpu.CompilerParams(
            dimension_semantics=("arbitrary", "arbitrary", "arbitrary"),
            vmem_limit_bytes=VMEM_LIMIT),
        name="attn",
    )(slopes, q, k, v)


def _tail_kernel(x_ref, attn_ref, sgu_ref, p_ref, ga_ref, wout_ref, gpm_ref, gpf_ref,
                 wgu_ref, wd_ref, gpo_ref, wpg_ref, bpg_ref, wpp_ref, o_ref):
    d_ff = wd_ref.shape[0]
    sub = x_ref.shape[0] // TAIL_CHAINS
    chains = [slice(c * sub, (c + 1) * sub) for c in range(TAIL_CHAINS)]

    def dot(a, w_ref):
        return jnp.dot(a, w_ref[...], preferred_element_type=F32)

    groups = []
    for rows in chains:
        attn = jnp.concatenate([attn_ref[s, rows, :] for s in range(attn_ref.shape[0])], axis=-1)
        an = _rms(attn, ga_ref[...]).astype(BF16)
        groups.append(jnp.concatenate([an, sgu_ref[rows, :]], axis=-1))
    mixed = [dot(g, wout_ref) for g in groups]
    pe = [dot(p_ref[rows, :].astype(BF16), wpp_ref) for rows in chains]
    h = [x_ref[rows, :] + _rms(m, gpm_ref[...]) for rows, m in zip(chains, mixed)]
    f = [_rms(hc, gpf_ref[...]).astype(BF16) for hc in h]
    gu = [dot(fc, wgu_ref) for fc in f]
    hid = [(jax.nn.silu(g[:, :d_ff]) * g[:, d_ff:]).astype(BF16) for g in gu]
    y = [dot(hc, wd_ref) for hc in hid]
    h = [hc + _rms(yc, gpo_ref[...]) for hc, yc in zip(h, y)]
    gate = [jax.nn.sigmoid(dot(hc.astype(BF16), wpg_ref) + bpg_ref[...]) for hc in h]
    for rows, hc, gc, pc in zip(chains, h, gate, pe):
        o_ref[rows, :] = hc + gc * pc


def _tail(x2, attn, sgu, p2, *params):
    n, d = x2.shape
    tm = TAIL_TILE

    def tile(a):
        return pl.BlockSpec((tm, a.shape[1]), lambda i: (i, 0))

    return pl.pallas_call(
        _tail_kernel,
        grid=(n // tm,),
        in_specs=[tile(x2), pl.BlockSpec((attn.shape[0], tm, LANES), lambda i: (0, i, 0)),
                  tile(sgu), tile(p2)] + [_const_spec(a.shape) for a in params],
        out_specs=pl.BlockSpec((tm, d), lambda i: (i, 0)),
        out_shape=jax.ShapeDtypeStruct((n, d), F32),
        compiler_params=pltpu.CompilerParams(dimension_semantics=("arbitrary",),
                                             vmem_limit_bytes=VMEM_LIMIT),
        name="tail",
    )(x2, attn, sgu, p2, *params)


def _group_major(w_in, attn_w, sgu_w):
    gd = sgu_w // N_SGU_GROUPS
    u0 = 3 * attn_w
    z0 = u0 + sgu_w
    cols = []
    for g in range(N_SGU_GROUPS):
        cols += [w_in[:, u0 + g * gd:u0 + (g + 1) * gd], w_in[:, z0 + g * gd:z0 + (g + 1) * gd]]
    return jnp.concatenate(cols + [w_in[:, :u0]], axis=1)


def kernel(x, p, ln_pre_mix, w_in, sgu_ln_g, sgu_ln_b, w_spatial, b_spatial, attn_out_norm,
           sgu_out_norm, w_out, ln_post_mix, ln_pre_ffn, w_gate_up, w_down, ln_post_ffn,
           w_pe_gate, b_pe_gate, w_pe_proj):
    batch, seq, d = x.shape
    depth = w_in.shape[0]
    attn_w = attn_out_norm.shape[1]
    sgu_w = sgu_out_norm.shape[1]
    n_heads = attn_w // HEAD_DIM
    slopes = 2.0 ** (-8.0 * (jnp.arange(n_heads, dtype=F32) + 1.0) / n_heads)

    def row(a):
        return a.reshape(1, -1)

    h = x.reshape(batch * seq, d)
    for i in range(depth):
        q, k, v, sgu = _proj_sgu(
            h, row(ln_pre_mix[i]), _group_major(w_in[i], attn_w, sgu_w).astype(BF16),
            row(sgu_ln_g[i]), row(sgu_ln_b[i]),
            w_spatial[i], b_spatial[i][:, :, None], row(sgu_out_norm[i]), attn_w, sgu_w)
        attn = _attention(q, k, v, slopes, batch, seq)
        h = _tail(h, attn, sgu, p[i].reshape(batch * seq, -1),
                  row(attn_out_norm[i]), w_out[i].astype(BF16), row(ln_post_mix[i]),
                  row(ln_pre_ffn[i]), w_gate_up[i].astype(BF16), w_down[i].astype(BF16),
                  row(ln_post_ffn[i]), w_pe_gate[i].astype(BF16), row(b_pe_gate[i]),
                  w_pe_proj[i].astype(BF16))
    return h.reshape(batch, seq, d)
```

```python
import functools

import jax
import jax.numpy as jnp
from jax import lax
from jax.experimental import pallas as pl
from jax.experimental.pallas import tpu as pltpu

F32 = jnp.float32
BF16 = jnp.bfloat16

HEAD_DIM = 64
QBLK = 128
WIN = 2 * QBLK
DILATIONS = (1, 4, 16)
SPAN = QBLK * DILATIONS[-1]
SGU_CHUNK = 128
N_SGU_GROUPS = 4
EPS = 1e-6
NEG = -1e30

LANES = 128
HEADS_PER_SLAB = LANES // HEAD_DIM

PROJ_TILE = 512
TAIL_TILE = 512
TAIL_CHAINS = 2
VMEM_LIMIT = 56 * 1024 * 1024


def _rms(x, g):
    return x * lax.rsqrt(jnp.mean(x * x, axis=-1, keepdims=True) + EPS) * g


def _const_spec(shape):
    zeros = (0,) * len(shape)
    return pl.BlockSpec(shape, lambda *_: zeros, pipeline_mode=pl.Buffered(1))


def _proj_sgu_kernel(x_ref, g_ref, w_ref, lng_ref, lnb_ref, ws_ref, bs_ref, gout_ref,
                     qn, kn, vn, qa, ka, va, qb, kb, vb, sgu_ref, tok_tmp, res_tmp,
                     *, attn_w, sgu_w):
    tm = x_ref.shape[0]
    gd = sgu_w // N_SGU_GROUPS
    a = _rms(x_ref[...], g_ref[...]).astype(BF16)

    def project(c0, width):
        return jnp.dot(a, w_ref[:, c0:c0 + width], preferred_element_type=F32)

    def emit_layouts(idx, n_ref, a_ref, b_ref, cols):
        for s in range(attn_w // LANES):
            slab = cols[:, s * LANES:(s + 1) * LANES]
            n_ref[s] = slab.astype(BF16)
            tok_tmp[idx, s] = slab
            for r in range(4):
                every4 = tok_tmp[idx, s, pl.ds(r, tm // 4, stride=4), :]
                a_ref[s, 0, r] = every4.astype(BF16)
                res_tmp[idx, s, r] = every4
                for r2 in range(4):
                    b_ref[s, 0, 4 * r2 + r] = (
                        res_tmp[idx, s, r, pl.ds(r2, tm // 16, stride=4), :].astype(BF16))

    row = lax.broadcasted_iota(jnp.int32, (SGU_CHUNK, SGU_CHUNK), 0)
    col = lax.broadcasted_iota(jnp.int32, (SGU_CHUNK, SGU_CHUNK), 1)
    causal = row >= col
    outs = []
    ssq = jnp.zeros((tm, 1), F32)
    uz_pairs = [project(0, 4 * gd)]
    for g in range(N_SGU_GROUPS):
        uz = uz_pairs[g // 2][:, (g % 2) * 2 * gd:(g % 2 + 1) * 2 * gd]
        if g == 0:
            uz_pairs.append(project(4 * gd, 4 * gd))
        elif g == 1:
            emit_layouts(0, qn, qa, qb, project(2 * sgu_w, attn_w) * (HEAD_DIM ** -0.5))
        elif g == 2:
            emit_layouts(1, kn, ka, kb, project(2 * sgu_w + attn_w, attn_w))
        elif g == 3:
            emit_layouts(2, vn, va, vb, project(2 * sgu_w + 2 * attn_w, attn_w))
        u = jax.nn.gelu(uz[:, :gd])
        z = jax.nn.gelu(uz[:, gd:])
        zc = z - jnp.mean(z, axis=-1, keepdims=True)
        zn = zc * lax.rsqrt(jnp.mean(zc * zc, axis=-1, keepdims=True) + EPS)
        zn = (zn * lng_ref[...] + lnb_ref[...]).astype(BF16)
        wm = jnp.where(causal, ws_ref[g], 0.0).astype(BF16)
        mixed = [jnp.dot(wm, zn[c * SGU_CHUNK:(c + 1) * SGU_CHUNK], preferred_element_type=F32)
                 + bs_ref[g] for c in range(tm // SGU_CHUNK)]
        o = u * jnp.concatenate(mixed, axis=0)
        ssq = ssq + jnp.sum(o * o, axis=-1, keepdims=True)
        outs.append(o)
    scale = lax.rsqrt(ssq / sgu_w + EPS)
    for g in range(N_SGU_GROUPS):
        sgu_ref[:, g * gd:(g + 1) * gd] = (outs[g] * scale * gout_ref[:, g * gd:(g + 1) * gd]).astype(BF16)


def _proj_sgu(x2, g_pre, w_in, ln_g, ln_b, w_sp, b_sp, g_sgu, attn_w, sgu_w):
    n, d = x2.shape
    tm = PROJ_TILE
    per_span = SPAN // tm
    n_slabs = attn_w // LANES
    n_shape = jax.ShapeDtypeStruct((n_slabs, n, LANES), BF16)
    a_shape = jax.ShapeDtypeStruct((n_slabs, n // SPAN, 4, SPAN // 4, LANES), BF16)
    b_shape = jax.ShapeDtypeStruct((n_slabs, n // SPAN, 16, SPAN // 16, LANES), BF16)
    n_spec = pl.BlockSpec((n_slabs, tm, LANES), lambda i: (0, i, 0))
    a_spec = pl.BlockSpec((n_slabs, 1, 4, tm // 4, LANES),
                          lambda i: (0, i // per_span, 0, i % per_span, 0))
    b_spec = pl.BlockSpec((n_slabs, 1, 16, tm // 16, LANES),
                          lambda i: (0, i // per_span, 0, i % per_span, 0))
    return pl.pallas_call(
        functools.partial(_proj_sgu_kernel, attn_w=attn_w, sgu_w=sgu_w),
        grid=(n // tm,),
        in_specs=[
            pl.BlockSpec((tm, d), lambda i: (i, 0)),
            _const_spec(g_pre.shape), _const_spec(w_in.shape), _const_spec(ln_g.shape),
            _const_spec(ln_b.shape), _const_spec(w_sp.shape), _const_spec(b_sp.shape),
            _const_spec(g_sgu.shape),
        ],
        out_specs=[n_spec] * 3 + [a_spec] * 3 + [b_spec] * 3
        + [pl.BlockSpec((tm, sgu_w), lambda i: (i, 0))],
        out_shape=[n_shape] * 3 + [a_shape] * 3 + [b_shape] * 3
        + [jax.ShapeDtypeStruct((n, sgu_w), BF16)],
        scratch_shapes=[pltpu.VMEM((3, n_slabs, tm, LANES), F32),
                        pltpu.VMEM((3, n_slabs, 4, tm // 4, LANES), F32)],
        compiler_params=pltpu.CompilerParams(dimension_semantics=("arbitrary",),
                                             vmem_limit_bytes=VMEM_LIMIT),
        name="proj_sgu",
    )(x2, g_pre, w_in, ln_g, ln_b, w_sp, b_sp, g_sgu)


A_BLOCKS = SPAN // 4 // QBLK


def _attn_kernel(slopes_ref, qn, kn, vn, knp, vnp, qa, ka, va, kap, vap, qb, kb, vb, kbp, vbp,
                 o_ref, bias_ref, ob, lb):
    slab = pl.program_id(1)
    span = pl.program_id(2)
    lane = lax.broadcasted_iota(jnp.int32, (QBLK, LANES), 1)
    head0 = lane < HEAD_DIM
    first = span == 0

    @pl.when(first)
    def _start_of_sequence():
        i = lax.broadcasted_iota(jnp.int32, (WIN, WIN), 0)
        j = lax.broadcasted_iota(jnp.int32, (WIN, WIN), 1)
        steps = QBLK + (i & (QBLK - 1)) - j
        valid = (steps >= 0) & (steps <= QBLK)
        slope = jnp.where(i < QBLK, slopes_ref[HEADS_PER_SLAB * slab],
                          slopes_ref[HEADS_PER_SLAB * slab + 1])
        for b, d in enumerate(DILATIONS):
            bias = jnp.where(valid, -slope * (steps * d).astype(F32), NEG)
            bias_ref[b, 1] = bias
            bias_ref[b, 0] = jnp.where(j >= QBLK, bias, NEG)

    not_first = jnp.where(first, 0, 1)
    ones = jnp.ones((WIN, LANES), BF16)

    def window(prev_blk, cur_ref, idx, nb):
        if nb > 0:
            return cur_ref[idx + (slice((nb - 1) * QBLK, (nb + 1) * QBLK), slice(None))]
        prev = jnp.where(first, jnp.zeros_like(prev_blk), prev_blk)
        return jnp.concatenate([prev, cur_ref[idx + (slice(0, QBLK), slice(None))]], axis=0)

    def block(b, q, kw, vw, nb):
        zero = jnp.zeros_like(q)
        qq = jnp.concatenate([jnp.where(head0, q, zero), jnp.where(head0, zero, q)], axis=0)
        s = lax.dot_general(qq, kw, (((1,), (1,)), ((), ())), preferred_element_type=F32)
        s = s + bias_ref[b, not_first if nb == 0 else 1]
        m = jnp.max(s, axis=1, keepdims=True)
        p = jnp.exp(s - m).astype(BF16)
        acc = jnp.dot(p, jnp.concatenate([vw, ones], axis=1), preferred_element_type=F32)
        o_t = jnp.where(head0, acc[0:QBLK, 0:LANES], acc[QBLK:WIN, 0:LANES])
        l_t = jnp.where(head0, acc[0:QBLK, LANES:], acc[QBLK:WIN, LANES:])
        m_t = jnp.where(head0, m[0:QBLK], m[QBLK:WIN])
        return o_t / l_t, m_t + jnp.log(l_t)

    for nb in range(SPAN // QBLK):
        q = qn[0, nb * QBLK:(nb + 1) * QBLK, :]
        o_t, lse_t = block(0, q, window(knp[0], kn, (0,), nb), window(vnp[0], vn, (0,), nb), nb)
        ob[0, nb * QBLK:(nb + 1) * QBLK, :] = o_t
        lb[0, nb * QBLK:(nb + 1) * QBLK, :] = lse_t
    for r in range(4):
        for nb in range(A_BLOCKS):
            q = qa[0, 0, r, nb * QBLK:(nb + 1) * QBLK, :]
            o_t, lse_t = block(1, q, window(kap[0, 0, r], ka, (0, 0, r), nb),
                               window(vap[0, 0, r], va, (0, 0, r), nb), nb)
            rows = slice((r * A_BLOCKS + nb) * QBLK, (r * A_BLOCKS + nb + 1) * QBLK)
            ob[1, rows, :] = o_t
            lb[1, rows, :] = lse_t
    for r in range(16):
        o_t, lse_t = block(2, qb[0, 0, r], window(kbp[0, 0, r], kb, (0, 0, r), 0),
                           window(vbp[0, 0, r], vb, (0, 0, r), 0), 0)
        rows = pl.ds((r % 4) * (SPAN // 4) + r // 4, QBLK, stride=4)
        ob[2, rows, :] = o_t
        lb[2, rows, :] = lse_t

    def merge(c, carry):
        rows = pl.ds(pl.multiple_of(c * WIN, WIN), WIN)
        tokens = pl.ds(c // 2 + (c % 2) * (4 * WIN), WIN, stride=4)
        l0, l1, l2 = lb[0, tokens, :], lb[1, rows, :], lb[2, rows, :]
        mx = jnp.maximum(jnp.maximum(l0, l1), l2)
        w0, w1, w2 = jnp.exp(l0 - mx), jnp.exp(l1 - mx), jnp.exp(l2 - mx)
        num = w0 * ob[0, tokens, :] + w1 * ob[1, rows, :] + w2 * ob[2, rows, :]
        o_ref[0, tokens, :] = num / (w0 + w1 + w2)
        return carry

    lax.fori_loop(0, SPAN // WIN, merge, 0)


def _attention(qkv_layouts, slopes, batch, seq):
    qn, kn, vn, qa, ka, va, qb, kb, vb = qkv_layouts
    n_slabs, n, _ = qn.shape
    assert seq % SPAN == 0
    spans = seq // SPAN

    def cur(block_shape):
        zeros = (0,) * (len(block_shape) - 2)
        return pl.BlockSpec(block_shape, lambda b, s, t: (s, b * spans + t) + zeros)

    n_spec = cur((1, SPAN, LANES))
    a_spec = cur((1, 1, 4, SPAN // 4, LANES))
    b_spec = cur((1, 1, 16, QBLK, LANES))
    np_spec = pl.BlockSpec((1, QBLK, LANES),
                           lambda b, s, t: (s, jnp.maximum((b * spans + t) * (SPAN // QBLK) - 1, 0), 0))
    ap_spec = pl.BlockSpec((1, 1, 4, QBLK, LANES),
                           lambda b, s, t: (s, jnp.maximum(b * spans + t - 1, 0), 0, A_BLOCKS - 1, 0))
    bp_spec = pl.BlockSpec((1, 1, 16, QBLK, LANES),
                           lambda b, s, t: (s, jnp.maximum(b * spans + t - 1, 0), 0, 0, 0))
    return pl.pallas_call(
        _attn_kernel,
        grid=(batch, n_slabs, spans),
        in_specs=[pl.BlockSpec(memory_space=pltpu.SMEM),
                  n_spec, n_spec, n_spec, np_spec, np_spec,
                  a_spec, a_spec, a_spec, ap_spec, ap_spec,
                  b_spec, b_spec, b_spec, bp_spec, bp_spec],
        out_specs=pl.BlockSpec((1, SPAN, LANES), lambda b, s, t: (s, b * spans + t, 0)),
        out_shape=jax.ShapeDtypeStruct((n_slabs, n, LANES), F32),
        scratch_shapes=[
            pltpu.VMEM((len(DILATIONS), 2, WIN, WIN), F32),
            pltpu.VMEM((len(DILATIONS), SPAN, LANES), F32),
            pltpu.VMEM((len(DILATIONS), SPAN, LANES), F32),
        ],
        compiler_params=pltpu.CompilerParams(
            dimension_semantics=("arbitrary", "arbitrary", "arbitrary"),
            vmem_limit_bytes=VMEM_LIMIT),
        name="attn",
    )(slopes, qn, kn, vn, kn, vn, qa, ka, va, ka, va, qb, kb, vb, kb, vb)


def _tail_kernel(x_ref, attn_ref, sgu_ref, p_ref, ga_ref, wout_ref, gpm_ref, gpf_ref,
                 wgu_ref, wd_ref, gpo_ref, wpg_ref, bpg_ref, wpp_ref, o_ref):
    d_ff = wd_ref.shape[0]
    sub = x_ref.shape[0] // TAIL_CHAINS
    chains = [slice(c * sub, (c + 1) * sub) for c in range(TAIL_CHAINS)]

    def dot(a, w_ref):
        return jnp.dot(a, w_ref[...], preferred_element_type=F32)

    groups = []
    for rows in chains:
        attn = jnp.concatenate([attn_ref[s, rows, :] for s in range(attn_ref.shape[0])], axis=-1)
        an = _rms(attn, ga_ref[...]).astype(BF16)
        groups.append(jnp.concatenate([an, sgu_ref[rows, :]], axis=-1))
    mixed = [dot(g, wout_ref) for g in groups]
    pe = [dot(p_ref[rows, :].astype(BF16), wpp_ref) for rows in chains]
    h = [x_ref[rows, :] + _rms(m, gpm_ref[...]) for rows, m in zip(chains, mixed)]
    f = [_rms(hc, gpf_ref[...]).astype(BF16) for hc in h]
    gu = [dot(fc, wgu_ref) for fc in f]
    hid = [(jax.nn.silu(g[:, :d_ff]) * g[:, d_ff:]).astype(BF16) for g in gu]
    y = [dot(hc, wd_ref) for hc in hid]
    h = [hc + _rms(yc, gpo_ref[...]) for hc, yc in zip(h, y)]
    gate = [jax.nn.sigmoid(dot(hc.astype(BF16), wpg_ref) + bpg_ref[...]) for hc in h]
    for rows, hc, gc, pc in zip(chains, h, gate, pe):
        o_ref[rows, :] = hc + gc * pc


def _tail(x2, attn, sgu, p2, *params):
    n, d = x2.shape
    tm = TAIL_TILE

    def tile(a):
        return pl.BlockSpec((tm, a.shape[1]), lambda i: (i, 0))

    return pl.pallas_call(
        _tail_kernel,
        grid=(n // tm,),
        in_specs=[tile(x2), pl.BlockSpec((attn.shape[0], tm, LANES), lambda i: (0, i, 0)),
                  tile(sgu), tile(p2)] + [_const_spec(a.shape) for a in params],
        out_specs=pl.BlockSpec((tm, d), lambda i: (i, 0)),
        out_shape=jax.ShapeDtypeStruct((n, d), F32),
        compiler_params=pltpu.CompilerParams(dimension_semantics=("arbitrary",),
                                             vmem_limit_bytes=VMEM_LIMIT),
        name="tail",
    )(x2, attn, sgu, p2, *params)


def _group_major(w_in, attn_w, sgu_w):
    gd = sgu_w // N_SGU_GROUPS
    u0 = 3 * attn_w
    z0 = u0 + sgu_w
    cols = []
    for g in range(N_SGU_GROUPS):
        cols += [w_in[:, u0 + g * gd:u0 + (g + 1) * gd], w_in[:, z0 + g * gd:z0 + (g + 1) * gd]]
    return jnp.concatenate(cols + [w_in[:, :u0]], axis=1)


def kernel(x, p, ln_pre_mix, w_in, sgu_ln_g, sgu_ln_b, w_spatial, b_spatial, attn_out_norm,
           sgu_out_norm, w_out, ln_post_mix, ln_pre_ffn, w_gate_up, w_down, ln_post_ffn,
           w_pe_gate, b_pe_gate, w_pe_proj):
    batch, seq, d = x.shape
    depth = w_in.shape[0]
    attn_w = attn_out_norm.shape[1]
    sgu_w = sgu_out_norm.shape[1]
    n_heads = attn_w // HEAD_DIM
    slopes = 2.0 ** (-8.0 * (jnp.arange(n_heads, dtype=F32) + 1.0) / n_heads)

    def row(a):
        return a.reshape(1, -1)

    h = x.reshape(batch * seq, d)
    for i in range(depth):
        *qkv_layouts, sgu = _proj_sgu(
            h, row(ln_pre_mix[i]), _group_major(w_in[i], attn_w, sgu_w).astype(BF16),
            row(sgu_ln_g[i]), row(sgu_ln_b[i]),
            w_spatial[i], b_spatial[i][:, :, None], row(sgu_out_norm[i]), attn_w, sgu_w)
        attn = _attention(qkv_layouts, slopes, batch, seq)
        h = _tail(h, attn, sgu, p[i].reshape(batch * seq, -1),
                  row(attn_out_norm[i]), w_out[i].astype(BF16), row(ln_post_mix[i]),
                  row(ln_pre_ffn[i]), w_gate_up[i].astype(BF16), w_down[i].astype(BF16),
                  row(ln_post_ffn[i]), w_pe_gate[i].astype(BF16), row(b_pe_gate[i]),
                  w_pe_proj[i].astype(BF16))
    return h.reshape(batch, seq, d)
```

```python
import functools

import jax
import jax.numpy as jnp
from jax import lax
from jax.experimental import pallas as pl
from jax.experimental.pallas import tpu as pltpu

F32 = jnp.float32
BF16 = jnp.bfloat16

HEAD_DIM = 64
QBLK = 128
WIN = 2 * QBLK
DILATIONS = (1, 4, 16)
SPAN = QBLK * DILATIONS[-1]
SGU_CHUNK = 128
N_SGU_GROUPS = 4
EPS = 1e-6
NEG = -1e30
LOG2E = 1.4426950408889634
Q_SCALE = HEAD_DIM ** -0.5 * LOG2E

LANES = 128
HEADS_PER_SLAB = LANES // HEAD_DIM

PROJ_TILE = 512
TAIL_TILE = 512
TAIL_CHAINS = 2
VMEM_LIMIT = 56 * 1024 * 1024


def _rms(x, g):
    return x * lax.rsqrt(jnp.mean(x * x, axis=-1, keepdims=True) + EPS) * g


def _const_spec(shape):
    zeros = (0,) * len(shape)
    return pl.BlockSpec(shape, lambda *_: zeros, pipeline_mode=pl.Buffered(1))


def _proj_sgu_kernel(x_ref, g_ref, w_ref, lng_ref, lnb_ref, ws_ref, bs_ref, gout_ref,
                     qn, kn, vn, qa, ka, va, qb, kb, vb, sgu_ref, tok_tmp, res_tmp,
                     *, attn_w, sgu_w):
    tm = x_ref.shape[0]
    gd = sgu_w // N_SGU_GROUPS
    a = _rms(x_ref[...], g_ref[...]).astype(BF16)

    def project(c0, width):
        return jnp.dot(a, w_ref[:, c0:c0 + width], preferred_element_type=F32)

    def emit_layouts(idx, n_ref, a_ref, b_ref, cols):
        for s in range(attn_w // LANES):
            slab = cols[:, s * LANES:(s + 1) * LANES]
            n_ref[s] = slab.astype(BF16)
            tok_tmp[idx, s] = slab
            for r in range(4):
                every4 = tok_tmp[idx, s, pl.ds(r, tm // 4, stride=4), :]
                a_ref[s, 0, r] = every4.astype(BF16)
                res_tmp[idx, s, r] = every4
                for r2 in range(4):
                    b_ref[s, 0, 4 * r2 + r] = (
                        res_tmp[idx, s, r, pl.ds(r2, tm // 16, stride=4), :].astype(BF16))

    row = lax.broadcasted_iota(jnp.int32, (SGU_CHUNK, SGU_CHUNK), 0)
    col = lax.broadcasted_iota(jnp.int32, (SGU_CHUNK, SGU_CHUNK), 1)
    causal = row >= col
    outs = []
    ssq = jnp.zeros((tm, 1), F32)
    uz_pairs = [project(0, 4 * gd)]
    for g in range(N_SGU_GROUPS):
        uz = uz_pairs[g // 2][:, (g % 2) * 2 * gd:(g % 2 + 1) * 2 * gd]
        if g == 0:
            uz_pairs.append(project(4 * gd, 4 * gd))
        elif g == 1:
            emit_layouts(0, qn, qa, qb, project(2 * sgu_w, attn_w) * Q_SCALE)
        elif g == 2:
            emit_layouts(1, kn, ka, kb, project(2 * sgu_w + attn_w, attn_w))
        elif g == 3:
            emit_layouts(2, vn, va, vb, project(2 * sgu_w + 2 * attn_w, attn_w))
        u = jax.nn.gelu(uz[:, :gd])
        z = jax.nn.gelu(uz[:, gd:])
        zc = z - jnp.mean(z, axis=-1, keepdims=True)
        zn = zc * lax.rsqrt(jnp.mean(zc * zc, axis=-1, keepdims=True) + EPS)
        zn = (zn * lng_ref[...] + lnb_ref[...]).astype(BF16)
        wm = jnp.where(causal, ws_ref[g], 0.0).astype(BF16)
        mixed = [jnp.dot(wm, zn[c * SGU_CHUNK:(c + 1) * SGU_CHUNK], preferred_element_type=F32)
                 + bs_ref[g] for c in range(tm // SGU_CHUNK)]
        o = u * jnp.concatenate(mixed, axis=0)
        ssq = ssq + jnp.sum(o * o, axis=-1, keepdims=True)
        outs.append(o)
    scale = lax.rsqrt(ssq / sgu_w + EPS)
    for g in range(N_SGU_GROUPS):
        sgu_ref[:, g * gd:(g + 1) * gd] = (outs[g] * scale * gout_ref[:, g * gd:(g + 1) * gd]).astype(BF16)


def _proj_sgu(x2, g_pre, w_in, ln_g, ln_b, w_sp, b_sp, g_sgu, attn_w, sgu_w):
    n, d = x2.shape
    tm = PROJ_TILE
    per_span = SPAN // tm
    n_slabs = attn_w // LANES
    n_shape = jax.ShapeDtypeStruct((n_slabs, n, LANES), BF16)
    a_shape = jax.ShapeDtypeStruct((n_slabs, n // SPAN, 4, SPAN // 4, LANES), BF16)
    b_shape = jax.ShapeDtypeStruct((n_slabs, n // SPAN, 16, SPAN // 16, LANES), BF16)
    n_spec = pl.BlockSpec((n_slabs, tm, LANES), lambda i: (0, i, 0))
    a_spec = pl.BlockSpec((n_slabs, 1, 4, tm // 4, LANES),
                          lambda i: (0, i // per_span, 0, i % per_span, 0))
    b_spec = pl.BlockSpec((n_slabs, 1, 16, tm // 16, LANES),
                          lambda i: (0, i // per_span, 0, i % per_span, 0))
    return pl.pallas_call(
        functools.partial(_proj_sgu_kernel, attn_w=attn_w, sgu_w=sgu_w),
        grid=(n // tm,),
        in_specs=[
            pl.BlockSpec((tm, d), lambda i: (i, 0)),
            _const_spec(g_pre.shape), _const_spec(w_in.shape), _const_spec(ln_g.shape),
            _const_spec(ln_b.shape), _const_spec(w_sp.shape), _const_spec(b_sp.shape),
            _const_spec(g_sgu.shape),
        ],
        out_specs=[n_spec] * 3 + [a_spec] * 3 + [b_spec] * 3
        + [pl.BlockSpec((tm, sgu_w), lambda i: (i, 0))],
        out_shape=[n_shape] * 3 + [a_shape] * 3 + [b_shape] * 3
        + [jax.ShapeDtypeStruct((n, sgu_w), BF16)],
        scratch_shapes=[pltpu.VMEM((3, n_slabs, tm, LANES), F32),
                        pltpu.VMEM((3, n_slabs, 4, tm // 4, LANES), F32)],
        compiler_params=pltpu.CompilerParams(dimension_semantics=("arbitrary",),
                                             vmem_limit_bytes=VMEM_LIMIT),
        name="proj_sgu",
    )(x2, g_pre, w_in, ln_g, ln_b, w_sp, b_sp, g_sgu)


A_BLOCKS = SPAN // 4 // QBLK


def _attn_kernel(slopes_ref, qn, kn, vn, knp, vnp, qa, ka, va, kap, vap, qb, kb, vb, kbp, vbp,
                 o_ref, bias_ref, ob, lb):
    slab = pl.program_id(1)
    span = pl.program_id(2)
    lane = lax.broadcasted_iota(jnp.int32, (QBLK, LANES), 1)
    head0 = lane < HEAD_DIM
    first = span == 0

    @pl.when(first)
    def _start_of_sequence():
        i = lax.broadcasted_iota(jnp.int32, (WIN, WIN), 0)
        j = lax.broadcasted_iota(jnp.int32, (WIN, WIN), 1)
        steps = QBLK + (i & (QBLK - 1)) - j
        valid = (steps >= 0) & (steps <= QBLK)
        slope = jnp.where(i < QBLK, slopes_ref[HEADS_PER_SLAB * slab],
                          slopes_ref[HEADS_PER_SLAB * slab + 1])
        for b, d in enumerate(DILATIONS):
            bias = jnp.where(valid, -(slope * LOG2E) * (steps * d).astype(F32), NEG)
            bias_ref[b, 1] = bias
            bias_ref[b, 0] = jnp.where(j >= QBLK, bias, NEG)

    not_first = jnp.where(first, 0, 1)
    ones = jnp.ones((WIN, LANES), BF16)

    def window(prev_blk, cur_ref, idx, nb):
        if nb > 0:
            return cur_ref[idx + (slice((nb - 1) * QBLK, (nb + 1) * QBLK), slice(None))]
        prev = jnp.where(first, jnp.zeros_like(prev_blk), prev_blk)
        return jnp.concatenate([prev, cur_ref[idx + (slice(0, QBLK), slice(None))]], axis=0)

    def block(b, q, kw, vw, nb):
        zero = jnp.zeros_like(q)
        qq = jnp.concatenate([jnp.where(head0, q, zero), jnp.where(head0, zero, q)], axis=0)
        s = lax.dot_general(qq, kw, (((1,), (1,)), ((), ())), preferred_element_type=F32)
        s = s + bias_ref[b, not_first if nb == 0 else 1]
        m = jnp.max(s, axis=1, keepdims=True)
        p = jnp.exp2(s - m).astype(BF16)
        acc = jnp.dot(p, jnp.concatenate([vw, ones], axis=1), preferred_element_type=F32)
        o_t = jnp.where(head0, acc[0:QBLK, 0:LANES], acc[QBLK:WIN, 0:LANES])
        l_t = jnp.where(head0, acc[0:QBLK, LANES:], acc[QBLK:WIN, LANES:])
        m_t = jnp.where(head0, m[0:QBLK], m[QBLK:WIN])
        return o_t / l_t, m_t + jnp.log2(l_t)

    for nb in range(SPAN // QBLK):
        q = qn[0, nb * QBLK:(nb + 1) * QBLK, :]
        o_t, lse_t = block(0, q, window(knp[0], kn, (0,), nb), window(vnp[0], vn, (0,), nb), nb)
        ob[0, nb * QBLK:(nb + 1) * QBLK, :] = o_t
        lb[0, nb * QBLK:(nb + 1) * QBLK, :] = lse_t
    for r in range(4):
        for nb in range(A_BLOCKS):
            q = qa[0, 0, r, nb * QBLK:(nb + 1) * QBLK, :]
            o_t, lse_t = block(1, q, window(kap[0, 0, r], ka, (0, 0, r), nb),
                               window(vap[0, 0, r], va, (0, 0, r), nb), nb)
            rows = slice((r * A_BLOCKS + nb) * QBLK, (r * A_BLOCKS + nb + 1) * QBLK)
            ob[1, rows, :] = o_t
            lb[1, rows, :] = lse_t
    for r in range(16):
        o_t, lse_t = block(2, qb[0, 0, r], window(kbp[0, 0, r], kb, (0, 0, r), 0),
                           window(vbp[0, 0, r], vb, (0, 0, r), 0), 0)
        rows = pl.ds((r % 4) * (SPAN // 4) + r // 4, QBLK, stride=4)
        ob[2, rows, :] = o_t
        lb[2, rows, :] = lse_t

    def merge(c, carry):
        rows = pl.ds(pl.multiple_of(c * WIN, WIN), WIN)
        tokens = pl.ds(c // 2 + (c % 2) * (4 * WIN), WIN, stride=4)
        l0, l1, l2 = lb[0, tokens, :], lb[1, rows, :], lb[2, rows, :]
        mx = jnp.maximum(jnp.maximum(l0, l1), l2)
        w0, w1, w2 = jnp.exp2(l0 - mx), jnp.exp2(l1 - mx), jnp.exp2(l2 - mx)
        num = w0 * ob[0, tokens, :] + w1 * ob[1, rows, :] + w2 * ob[2, rows, :]
        o_ref[0, tokens, :] = num / (w0 + w1 + w2)
        return carry

    lax.fori_loop(0, SPAN // WIN, merge, 0)


def _attention(qkv_layouts, slopes, batch, seq):
    qn, kn, vn, qa, ka, va, qb, kb, vb = qkv_layouts
    n_slabs, n, _ = qn.shape
    assert seq % SPAN == 0
    spans = seq // SPAN

    def cur(block_shape):
        zeros = (0,) * (len(block_shape) - 2)
        return pl.BlockSpec(block_shape, lambda b, s, t: (s, b * spans + t) + zeros)

    n_spec = cur((1, SPAN, LANES))
    a_spec = cur((1, 1, 4, SPAN // 4, LANES))
    b_spec = cur((1, 1, 16, QBLK, LANES))
    np_spec = pl.BlockSpec((1, QBLK, LANES),
                           lambda b, s, t: (s, jnp.maximum((b * spans + t) * (SPAN // QBLK) - 1, 0), 0))
    ap_spec = pl.BlockSpec((1, 1, 4, QBLK, LANES),
                           lambda b, s, t: (s, jnp.maximum(b * spans + t - 1, 0), 0, A_BLOCKS - 1, 0))
    bp_spec = pl.BlockSpec((1, 1, 16, QBLK, LANES),
                           lambda b, s, t: (s, jnp.maximum(b * spans + t - 1, 0), 0, 0, 0))
    return pl.pallas_call(
        _attn_kernel,
        grid=(batch, n_slabs, spans),
        in_specs=[pl.BlockSpec(memory_space=pltpu.SMEM),
                  n_spec, n_spec, n_spec, np_spec, np_spec,
                  a_spec, a_spec, a_spec, ap_spec, ap_spec,
                  b_spec, b_spec, b_spec, bp_spec, bp_spec],
        out_specs=pl.BlockSpec((1, SPAN, LANES), lambda b, s, t: (s, b * spans + t, 0)),
        out_shape=jax.ShapeDtypeStruct((n_slabs, n, LANES), F32),
        scratch_shapes=[
            pltpu.VMEM((len(DILATIONS), 2, WIN, WIN), F32),
            pltpu.VMEM((len(DILATIONS), SPAN, LANES), F32),
            pltpu.VMEM((len(DILATIONS), SPAN, LANES), F32),
        ],
        compiler_params=pltpu.CompilerParams(
            dimension_semantics=("arbitrary", "arbitrary", "arbitrary"),
            vmem_limit_bytes=VMEM_LIMIT),
        name="attn",
    )(slopes, qn, kn, vn, kn, vn, qa, ka, va, ka, va, qb, kb, vb, kb, vb)


def _tail_kernel(x_ref, attn_ref, sgu_ref, p_ref, ga_ref, wout_ref, gpm_ref, gpf_ref,
                 wgu_ref, wd_ref, gpo_ref, wpg_ref, bpg_ref, wpp_ref, o_ref):
    d_ff = wd_ref.shape[0]
    sub = x_ref.shape[0] // TAIL_CHAINS
    chains = [slice(c * sub, (c + 1) * sub) for c in range(TAIL_CHAINS)]

    def dot(a, w_ref):
        return jnp.dot(a, w_ref[...], preferred_element_type=F32)

    groups = []
    for rows in chains:
        attn = jnp.concatenate([attn_ref[s, rows, :] for s in range(attn_ref.shape[0])], axis=-1)
        an = _rms(attn, ga_ref[...]).astype(BF16)
        groups.append(jnp.concatenate([an, sgu_ref[rows, :]], axis=-1))
    mixed = [dot(g, wout_ref) for g in groups]
    pe = [dot(p_ref[rows, :].astype(BF16), wpp_ref) for rows in chains]
    h = [x_ref[rows, :] + _rms(m, gpm_ref[...]) for rows, m in zip(chains, mixed)]
    f = [_rms(hc, gpf_ref[...]).astype(BF16) for hc in h]
    gu = [dot(fc, wgu_ref) for fc in f]
    hid = [(jax.nn.silu(g[:, :d_ff]) * g[:, d_ff:]).astype(BF16) for g in gu]
    y = [dot(hc, wd_ref) for hc in hid]
    h = [hc + _rms(yc, gpo_ref[...]) for hc, yc in zip(h, y)]
    gate = [jax.nn.sigmoid(dot(hc.astype(BF16), wpg_ref) + bpg_ref[...]) for hc in h]
    for rows, hc, gc, pc in zip(chains, h, gate, pe):
        o_ref[rows, :] = hc + gc * pc


def _tail(x2, attn, sgu, p2, *params):
    n, d = x2.shape
    tm = TAIL_TILE

    def tile(a):
        return pl.BlockSpec((tm, a.shape[1]), lambda i: (i, 0))

    return pl.pallas_call(
        _tail_kernel,
        grid=(n // tm,),
        in_specs=[tile(x2), pl.BlockSpec((attn.shape[0], tm, LANES), lambda i: (0, i, 0)),
                  tile(sgu), tile(p2)] + [_const_spec(a.shape) for a in params],
        out_specs=pl.BlockSpec((tm, d), lambda i: (i, 0)),
        out_shape=jax.ShapeDtypeStruct((n, d), F32),
        compiler_params=pltpu.CompilerParams(dimension_semantics=("arbitrary",),
                                             vmem_limit_bytes=VMEM_LIMIT),
        name="tail",
    )(x2, attn, sgu, p2, *params)


def _group_major(w_in, attn_w, sgu_w):
    gd = sgu_w // N_SGU_GROUPS
    u0 = 3 * attn_w
    z0 = u0 + sgu_w
    cols = []
    for g in range(N_SGU_GROUPS):
        cols += [w_in[:, u0 + g * gd:u0 + (g + 1) * gd], w_in[:, z0 + g * gd:z0 + (g + 1) * gd]]
    return jnp.concatenate(cols + [w_in[:, :u0]], axis=1)


def kernel(x, p, ln_pre_mix, w_in, sgu_ln_g, sgu_ln_b, w_spatial, b_spatial, attn_out_norm,
           sgu_out_norm, w_out, ln_post_mix, ln_pre_ffn, w_gate_up, w_down, ln_post_ffn,
           w_pe_gate, b_pe_gate, w_pe_proj):
    batch, seq, d = x.shape
    depth = w_in.shape[0]
    attn_w = attn_out_norm.shape[1]
    sgu_w = sgu_out_norm.shape[1]
    n_heads = attn_w // HEAD_DIM
    slopes = 2.0 ** (-8.0 * (jnp.arange(n_heads, dtype=F32) + 1.0) / n_heads)

    def row(a):
        return a.reshape(1, -1)

    h = x.reshape(batch * seq, d)
    for i in range(depth):
        *qkv_layouts, sgu = _proj_sgu(
            h, row(ln_pre_mix[i]), _group_major(w_in[i], attn_w, sgu_w).astype(BF16),
            row(sgu_ln_g[i]), row(sgu_ln_b[i]),
            w_spatial[i], b_spatial[i][:, :, None], row(sgu_out_norm[i]), attn_w, sgu_w)
        attn = _attention(qkv_layouts, slopes, batch, seq)
        h = _tail(h, attn, sgu, p[i].reshape(batch * seq, -1),
                  row(attn_out_norm[i]), w_out[i].astype(BF16), row(ln_post_mix[i]),
                  row(ln_pre_ffn[i]), w_gate_up[i].astype(BF16), w_down[i].astype(BF16),
                  row(ln_post_ffn[i]), w_pe_gate[i].astype(BF16), row(b_pe_gate[i]),
                  w_pe_proj[i].astype(BF16))
    return h.reshape(batch, seq, d)
```

```python
import functools

import jax
import jax.numpy as jnp
from jax import lax
from jax.experimental import pallas as pl
from jax.experimental.pallas import tpu as pltpu

F32 = jnp.float32
BF16 = jnp.bfloat16

HEAD_DIM = 64
QBLK = 128
WIN = 2 * QBLK
DILATIONS = (1, 4, 16)
SPAN = QBLK * DILATIONS[-1]
SGU_CHUNK = 128
N_SGU_GROUPS = 4
EPS = 1e-6
NEG = -1e30
LOG2E = 1.4426950408889634
Q_SCALE = HEAD_DIM ** -0.5 * LOG2E

LANES = 128
HEADS_PER_SLAB = LANES // HEAD_DIM

PROJ_TILE = 512
TAIL_TILE = 512
TAIL_CHAINS = 2
VMEM_LIMIT = 56 * 1024 * 1024


def _rms(x, g):
    return x * lax.rsqrt(jnp.mean(x * x, axis=-1, keepdims=True) + EPS) * g


def _const_spec(shape):
    zeros = (0,) * len(shape)
    return pl.BlockSpec(shape, lambda *_: zeros, pipeline_mode=pl.Buffered(1))


def _proj_sgu_kernel(x_ref, g_ref, w_ref, lng_ref, lnb_ref, ws_ref, bs_ref, gout_ref,
                     qn, kn, vn, qa, ka, va, qb, kb, vb, sgu_ref, tok_even, tok_odd, res_tmp,
                     *, attn_w, sgu_w):
    i = pl.program_id(0)

    @pl.when(i == 0)
    def _no_previous_tile():
        tok_odd[...] = jnp.zeros(tok_odd.shape, tok_odd.dtype)

    body = functools.partial(_proj_sgu_step, x_ref, g_ref, w_ref, lng_ref, lnb_ref, ws_ref, bs_ref,
                             gout_ref, ((qn, qa, qb), (kn, ka, kb), (vn, va, vb)), sgu_ref, res_tmp,
                             attn_w=attn_w, sgu_w=sgu_w)
    pl.when(i % 2 == 0)(functools.partial(body, tok_even, tok_odd))
    pl.when(i % 2 == 1)(functools.partial(body, tok_odd, tok_even))


def _proj_sgu_step(x_ref, g_ref, w_ref, lng_ref, lnb_ref, ws_ref, bs_ref, gout_ref, layouts, sgu_ref,
                   res_tmp, tok_cur, tok_prev, *, attn_w, sgu_w):
    tm = x_ref.shape[0]
    gd = sgu_w // N_SGU_GROUPS

    def relayout_previous(idx):
        n_ref, a_ref, b_ref = layouts[idx]
        for s in range(attn_w // LANES):
            n_ref[s] = tok_prev[idx, s].astype(BF16)
            for r in range(4):
                every4 = tok_prev[idx, s, pl.ds(r, tm // 4, stride=4), :]
                a_ref[s, 0, r] = every4.astype(BF16)
                res_tmp[idx, s, r] = every4
                for r2 in range(4):
                    b_ref[s, 0, 4 * r2 + r] = (
                        res_tmp[idx, s, r, pl.ds(r2, tm // 16, stride=4), :].astype(BF16))

    a = _rms(x_ref[...], g_ref[...]).astype(BF16)

    def project(c0, width):
        return jnp.dot(a, w_ref[:, c0:c0 + width], preferred_element_type=F32)

    def keep(idx, cols):
        for s in range(attn_w // LANES):
            tok_cur[idx, s] = cols[:, s * LANES:(s + 1) * LANES]

    row = lax.broadcasted_iota(jnp.int32, (SGU_CHUNK, SGU_CHUNK), 0)
    col = lax.broadcasted_iota(jnp.int32, (SGU_CHUNK, SGU_CHUNK), 1)
    causal = row >= col
    outs = []
    ssq = jnp.zeros((tm, 1), F32)
    uz_pairs = [project(0, 4 * gd)]
    relayout_previous(0)
    for g in range(N_SGU_GROUPS):
        uz = uz_pairs[g // 2][:, (g % 2) * 2 * gd:(g % 2 + 1) * 2 * gd]
        if g == 0:
            uz_pairs.append(project(4 * gd, 4 * gd))
            relayout_previous(1)
        elif g == 1:
            relayout_previous(2)
            keep(0, project(2 * sgu_w, attn_w) * Q_SCALE)
        elif g == 2:
            keep(1, project(2 * sgu_w + attn_w, attn_w))
        elif g == 3:
            keep(2, project(2 * sgu_w + 2 * attn_w, attn_w))
        u = jax.nn.gelu(uz[:, :gd])
        z = jax.nn.gelu(uz[:, gd:])
        zc = z - jnp.mean(z, axis=-1, keepdims=True)
        zn = zc * lax.rsqrt(jnp.mean(zc * zc, axis=-1, keepdims=True) + EPS)
        zn = (zn * lng_ref[...] + lnb_ref[...]).astype(BF16)
        wm = jnp.where(causal, ws_ref[g], 0.0).astype(BF16)
        mixed = [jnp.dot(wm, zn[c * SGU_CHUNK:(c + 1) * SGU_CHUNK], preferred_element_type=F32)
                 + bs_ref[g] for c in range(tm // SGU_CHUNK)]
        o = u * jnp.concatenate(mixed, axis=0)
        ssq = ssq + jnp.sum(o * o, axis=-1, keepdims=True)
        outs.append(o)
    scale = lax.rsqrt(ssq / sgu_w + EPS)
    for g in range(N_SGU_GROUPS):
        sgu_ref[:, g * gd:(g + 1) * gd] = (outs[g] * scale * gout_ref[:, g * gd:(g + 1) * gd]).astype(BF16)


def _proj_sgu(x2, g_pre, w_in, ln_g, ln_b, w_sp, b_sp, g_sgu, attn_w, sgu_w):
    n, d = x2.shape
    tm = PROJ_TILE
    per_span = SPAN // tm
    n_slabs = attn_w // LANES
    n_shape = jax.ShapeDtypeStruct((n_slabs, n, LANES), BF16)
    a_shape = jax.ShapeDtypeStruct((n_slabs, n // SPAN, 4, SPAN // 4, LANES), BF16)
    b_shape = jax.ShapeDtypeStruct((n_slabs, n // SPAN, 16, SPAN // 16, LANES), BF16)
    tiles = n // tm

    def cur(i):
        return jnp.minimum(i, tiles - 1)

    def prev(i):
        return jnp.maximum(i - 1, 0)

    n_spec = pl.BlockSpec((n_slabs, tm, LANES), lambda i: (0, prev(i), 0))
    a_spec = pl.BlockSpec((n_slabs, 1, 4, tm // 4, LANES),
                          lambda i: (0, prev(i) // per_span, 0, prev(i) % per_span, 0))
    b_spec = pl.BlockSpec((n_slabs, 1, 16, tm // 16, LANES),
                          lambda i: (0, prev(i) // per_span, 0, prev(i) % per_span, 0))
    return pl.pallas_call(
        functools.partial(_proj_sgu_kernel, attn_w=attn_w, sgu_w=sgu_w),
        grid=(tiles + 1,),
        in_specs=[
            pl.BlockSpec((tm, d), lambda i: (cur(i), 0)),
            _const_spec(g_pre.shape), _const_spec(w_in.shape), _const_spec(ln_g.shape),
            _const_spec(ln_b.shape), _const_spec(w_sp.shape), _const_spec(b_sp.shape),
            _const_spec(g_sgu.shape),
        ],
        out_specs=[n_spec] * 3 + [a_spec] * 3 + [b_spec] * 3
        + [pl.BlockSpec((tm, sgu_w), lambda i: (cur(i), 0))],
        out_shape=[n_shape] * 3 + [a_shape] * 3 + [b_shape] * 3
        + [jax.ShapeDtypeStruct((n, sgu_w), BF16)],
        scratch_shapes=[pltpu.VMEM((3, n_slabs, tm, LANES), F32),
                        pltpu.VMEM((3, n_slabs, tm, LANES), F32),
                        pltpu.VMEM((3, n_slabs, 4, tm // 4, LANES), F32)],
        compiler_params=pltpu.CompilerParams(dimension_semantics=("arbitrary",),
                                             vmem_limit_bytes=VMEM_LIMIT),
        name="proj_sgu",
    )(x2, g_pre, w_in, ln_g, ln_b, w_sp, b_sp, g_sgu)


A_BLOCKS = SPAN // 4 // QBLK


def _attn_kernel(slopes_ref, qn, kn, vn, knp, vnp, qa, ka, va, kap, vap, qb, kb, vb, kbp, vbp,
                 o_ref, bias_ref, ob, lb):
    slab = pl.program_id(1)
    span = pl.program_id(2)
    lane = lax.broadcasted_iota(jnp.int32, (QBLK, LANES), 1)
    head0 = lane < HEAD_DIM
    first = span == 0

    @pl.when(first)
    def _start_of_sequence():
        i = lax.broadcasted_iota(jnp.int32, (WIN, WIN), 0)
        j = lax.broadcasted_iota(jnp.int32, (WIN, WIN), 1)
        steps = QBLK + (i & (QBLK - 1)) - j
        valid = (steps >= 0) & (steps <= QBLK)
        slope = jnp.where(i < QBLK, slopes_ref[HEADS_PER_SLAB * slab],
                          slopes_ref[HEADS_PER_SLAB * slab + 1])
        for b, d in enumerate(DILATIONS):
            bias = jnp.where(valid, -(slope * LOG2E) * (steps * d).astype(F32), NEG)
            bias_ref[b, 1] = bias
            bias_ref[b, 0] = jnp.where(j >= QBLK, bias, NEG)

    not_first = jnp.where(first, 0, 1)
    ones = jnp.ones((WIN, LANES), BF16)

    def window(prev_blk, cur_ref, idx, nb):
        if nb > 0:
            return cur_ref[idx + (slice((nb - 1) * QBLK, (nb + 1) * QBLK), slice(None))]
        prev = jnp.where(first, jnp.zeros_like(prev_blk), prev_blk)
        return jnp.concatenate([prev, cur_ref[idx + (slice(0, QBLK), slice(None))]], axis=0)

    def block(b, q, kw, vw, nb):
        zero = jnp.zeros_like(q)
        qq = jnp.concatenate([jnp.where(head0, q, zero), jnp.where(head0, zero, q)], axis=0)
        s = lax.dot_general(qq, kw, (((1,), (1,)), ((), ())), preferred_element_type=F32)
        s = s + bias_ref[b, not_first if nb == 0 else 1]
        m = jnp.max(s, axis=1, keepdims=True)
        p = jnp.exp2(s - m).astype(BF16)
        acc = jnp.dot(p, jnp.concatenate([vw, ones], axis=1), preferred_element_type=F32)
        o_t = jnp.where(head0, acc[0:QBLK, 0:LANES], acc[QBLK:WIN, 0:LANES])
        l_t = jnp.where(head0, acc[0:QBLK, LANES:], acc[QBLK:WIN, LANES:])
        m_t = jnp.where(head0, m[0:QBLK], m[QBLK:WIN])
        return o_t / l_t, m_t + jnp.log2(l_t)

    for nb in range(SPAN // QBLK):
        q = qn[0, nb * QBLK:(nb + 1) * QBLK, :]
        o_t, lse_t = block(0, q, window(knp[0], kn, (0,), nb), window(vnp[0], vn, (0,), nb), nb)
        ob[0, nb * QBLK:(nb + 1) * QBLK, :] = o_t
        lb[0, nb * QBLK:(nb + 1) * QBLK, :] = lse_t
    for r in range(4):
        for nb in range(A_BLOCKS):
            q = qa[0, 0, r, nb * QBLK:(nb + 1) * QBLK, :]
            o_t, lse_t = block(1, q, window(kap[0, 0, r], ka, (0, 0, r), nb),
                               window(vap[0, 0, r], va, (0, 0, r), nb), nb)
            rows = slice((r * A_BLOCKS + nb) * QBLK, (r * A_BLOCKS + nb + 1) * QBLK)
            ob[1, rows, :] = o_t
            lb[1, rows, :] = lse_t
    for r in range(16):
        o_t, lse_t = block(2, qb[0, 0, r], window(kbp[0, 0, r], kb, (0, 0, r), 0),
                           window(vbp[0, 0, r], vb, (0, 0, r), 0), 0)
        rows = pl.ds((r % 4) * (SPAN // 4) + r // 4, QBLK, stride=4)
        ob[2, rows, :] = o_t
        lb[2, rows, :] = lse_t

    def merge(c, carry):
        rows = pl.ds(pl.multiple_of(c * WIN, WIN), WIN)
        tokens = pl.ds(c // 2 + (c % 2) * (4 * WIN), WIN, stride=4)
        l0, l1, l2 = lb[0, tokens, :], lb[1, rows, :], lb[2, rows, :]
        mx = jnp.maximum(jnp.maximum(l0, l1), l2)
        w0, w1, w2 = jnp.exp2(l0 - mx), jnp.exp2(l1 - mx), jnp.exp2(l2 - mx)
        num = w0 * ob[0, tokens, :] + w1 * ob[1, rows, :] + w2 * ob[2, rows, :]
        o_ref[0, tokens, :] = num / (w0 + w1 + w2)
        return carry

    lax.fori_loop(0, SPAN // WIN, merge, 0)


def _attention(qkv_layouts, slopes, batch, seq):
    qn, kn, vn, qa, ka, va, qb, kb, vb = qkv_layouts
    n_slabs, n, _ = qn.shape
    assert seq % SPAN == 0
    spans = seq // SPAN

    def cur(block_shape):
        zeros = (0,) * (len(block_shape) - 2)
        return pl.BlockSpec(block_shape, lambda b, s, t: (s, b * spans + t) + zeros)

    n_spec = cur((1, SPAN, LANES))
    a_spec = cur((1, 1, 4, SPAN // 4, LANES))
    b_spec = cur((1, 1, 16, QBLK, LANES))
    np_spec = pl.BlockSpec((1, QBLK, LANES),
                           lambda b, s, t: (s, jnp.maximum((b * spans + t) * (SPAN // QBLK) - 1, 0), 0))
    ap_spec = pl.BlockSpec((1, 1, 4, QBLK, LANES),
                           lambda b, s, t: (s, jnp.maximum(b * spans + t - 1, 0), 0, A_BLOCKS - 1, 0))
    bp_spec = pl.BlockSpec((1, 1, 16, QBLK, LANES),
                           lambda b, s, t: (s, jnp.maximum(b * spans + t - 1, 0), 0, 0, 0))
    return pl.pallas_call(
        _attn_kernel,
        grid=(batch, n_slabs, spans),
        in_specs=[pl.BlockSpec(memory_space=pltpu.SMEM),
                  n_spec, n_spec, n_spec, np_spec, np_spec,
                  a_spec, a_spec, a_spec, ap_spec, ap_spec,
                  b_spec, b_spec, b_spec, bp_spec, bp_spec],
        out_specs=pl.BlockSpec((1, SPAN, LANES), lambda b, s, t: (s, b * spans + t, 0)),
        out_shape=jax.ShapeDtypeStruct((n_slabs, n, LANES), F32),
        scratch_shapes=[
            pltpu.VMEM((len(DILATIONS), 2, WIN, WIN), F32),
            pltpu.VMEM((len(DILATIONS), SPAN, LANES), F32),
            pltpu.VMEM((len(DILATIONS), SPAN, LANES), F32),
        ],
        compiler_params=pltpu.CompilerParams(
            dimension_semantics=("arbitrary", "arbitrary", "arbitrary"),
            vmem_limit_bytes=VMEM_LIMIT),
        name="attn",
    )(slopes, qn, kn, vn, kn, vn, qa, ka, va, ka, va, qb, kb, vb, kb, vb)


def _tail_kernel(x_ref, attn_ref, sgu_ref, p_ref, ga_ref, wout_ref, gpm_ref, gpf_ref,
                 wgu_ref, wd_ref, gpo_ref, wpg_ref, bpg_ref, wpp_ref, o_ref):
    d_ff = wd_ref.shape[0]
    sub = x_ref.shape[0] // TAIL_CHAINS
    chains = [slice(c * sub, (c + 1) * sub) for c in range(TAIL_CHAINS)]

    def dot(a, w_ref):
        return jnp.dot(a, w_ref[...], preferred_element_type=F32)

    groups = []
    for rows in chains:
        attn = jnp.concatenate([attn_ref[s, rows, :] for s in range(attn_ref.shape[0])], axis=-1)
        an = _rms(attn, ga_ref[...]).astype(BF16)
        groups.append(jnp.concatenate([an, sgu_ref[rows, :]], axis=-1))
    mixed = [dot(g, wout_ref) for g in groups]
    pe = [dot(p_ref[rows, :].astype(BF16), wpp_ref) for rows in chains]
    h = [x_ref[rows, :] + _rms(m, gpm_ref[...]) for rows, m in zip(chains, mixed)]
    f = [_rms(hc, gpf_ref[...]).astype(BF16) for hc in h]
    gu = [dot(fc, wgu_ref) for fc in f]
    hid = [(jax.nn.silu(g[:, :d_ff]) * g[:, d_ff:]).astype(BF16) for g in gu]
    y = [dot(hc, wd_ref) for hc in hid]
    h = [hc + _rms(yc, gpo_ref[...]) for hc, yc in zip(h, y)]
    gate = [jax.nn.sigmoid(dot(hc.astype(BF16), wpg_ref) + bpg_ref[...]) for hc in h]
    for rows, hc, gc, pc in zip(chains, h, gate, pe):
        o_ref[rows, :] = hc + gc * pc


def _tail(x2, attn, sgu, p2, *params):
    n, d = x2.shape
    tm = TAIL_TILE

    def tile(a):
        return pl.BlockSpec((tm, a.shape[1]), lambda i: (i, 0))

    return pl.pallas_call(
        _tail_kernel,
        grid=(n // tm,),
        in_specs=[tile(x2), pl.BlockSpec((attn.shape[0], tm, LANES), lambda i: (0, i, 0)),
                  tile(sgu), tile(p2)] + [_const_spec(a.shape) for a in params],
        out_specs=pl.BlockSpec((tm, d), lambda i: (i, 0)),
        out_shape=jax.ShapeDtypeStruct((n, d), F32),
        compiler_params=pltpu.CompilerParams(dimension_semantics=("arbitrary",),
                                             vmem_limit_bytes=VMEM_LIMIT),
        name="tail",
    )(x2, attn, sgu, p2, *params)


def _group_major(w_in, attn_w, sgu_w):
    gd = sgu_w // N_SGU_GROUPS
    u0 = 3 * attn_w
    z0 = u0 + sgu_w
    cols = []
    for g in range(N_SGU_GROUPS):
        cols += [w_in[:, u0 + g * gd:u0 + (g + 1) * gd], w_in[:, z0 + g * gd:z0 + (g + 1) * gd]]
    return jnp.concatenate(cols + [w_in[:, :u0]], axis=1)


def kernel(x, p, ln_pre_mix, w_in, sgu_ln_g, sgu_ln_b, w_spatial, b_spatial, attn_out_norm,
           sgu_out_norm, w_out, ln_post_mix, ln_pre_ffn, w_gate_up, w_down, ln_post_ffn,
           w_pe_gate, b_pe_gate, w_pe_proj):
    batch, seq, d = x.shape
    depth = w_in.shape[0]
    attn_w = attn_out_norm.shape[1]
    sgu_w = sgu_out_norm.shape[1]
    n_heads = attn_w // HEAD_DIM
    slopes = 2.0 ** (-8.0 * (jnp.arange(n_heads, dtype=F32) + 1.0) / n_heads)

    def row(a):
        return a.reshape(1, -1)

    h = x.reshape(batch * seq, d)
    for i in range(depth):
        *qkv_layouts, sgu = _proj_sgu(
            h, row(ln_pre_mix[i]), _group_major(w_in[i], attn_w, sgu_w).astype(BF16),
            row(sgu_ln_g[i]), row(sgu_ln_b[i]),
            w_spatial[i], b_spatial[i][:, :, None], row(sgu_out_norm[i]), attn_w, sgu_w)
        attn = _attention(qkv_layouts, slopes, batch, seq)
        h = _tail(h, attn, sgu, p[i].reshape(batch * seq, -1),
                  row(attn_out_norm[i]), w_out[i].astype(BF16), row(ln_post_mix[i]),
                  row(ln_pre_ffn[i]), w_gate_up[i].astype(BF16), w_down[i].astype(BF16),
                  row(ln_post_ffn[i]), w_pe_gate[i].astype(BF16), row(b_pe_gate[i]),
                  w_pe_proj[i].astype(BF16))
    return h.reshape(batch, seq, d)
```

```python
import functools

import jax
import jax.numpy as jnp
from jax import lax
from jax.experimental import pallas as pl
from jax.experimental.pallas import tpu as pltpu

F32 = jnp.float32
BF16 = jnp.bfloat16

HEAD_DIM = 64
QBLK = 128
WIN = 2 * QBLK
DILATIONS = (1, 4, 16)
SPAN = QBLK * DILATIONS[-1]
SGU_CHUNK = 128
N_SGU_GROUPS = 4
EPS = 1e-6
NEG = -1e30
LOG2E = 1.4426950408889634
Q_SCALE = HEAD_DIM ** -0.5 * LOG2E

LANES = 128
HEADS_PER_SLAB = LANES // HEAD_DIM

PROJ_TILE = 512
TAIL_TILE = 512
TAIL_SPLITS = (256,)
VMEM_LIMIT = 56 * 1024 * 1024


def _rms(x, g):
    return x * lax.rsqrt(jnp.mean(x * x, axis=-1, keepdims=True) + EPS) * g


def _const_spec(shape):
    zeros = (0,) * len(shape)
    return pl.BlockSpec(shape, lambda *_: zeros, pipeline_mode=pl.Buffered(1))


def _proj_sgu_kernel(x_ref, g_ref, w_ref, lng_ref, lnb_ref, ws_ref, bs_ref, gout_ref,
                     qn, kn, vn, qa, ka, va, qb, kb, vb, sgu_ref, tok_tmp, res_tmp,
                     *, attn_w, sgu_w):
    tm = x_ref.shape[0]
    gd = sgu_w // N_SGU_GROUPS
    a = _rms(x_ref[...], g_ref[...]).astype(BF16)

    def project(c0, width):
        return jnp.dot(a, w_ref[:, c0:c0 + width], preferred_element_type=F32)

    def emit_layouts(idx, n_ref, a_ref, b_ref, cols):
        for s in range(attn_w // LANES):
            slab = cols[:, s * LANES:(s + 1) * LANES]
            n_ref[s] = slab.astype(BF16)
            tok_tmp[idx, s] = slab
            for r in range(4):
                every4 = tok_tmp[idx, s, pl.ds(r, tm // 4, stride=4), :]
                a_ref[s, 0, r] = every4.astype(BF16)
                res_tmp[idx, s, r] = every4
                for r2 in range(4):
                    b_ref[s, 0, 4 * r2 + r] = (
                        res_tmp[idx, s, r, pl.ds(r2, tm // 16, stride=4), :].astype(BF16))

    row = lax.broadcasted_iota(jnp.int32, (SGU_CHUNK, SGU_CHUNK), 0)
    col = lax.broadcasted_iota(jnp.int32, (SGU_CHUNK, SGU_CHUNK), 1)
    causal = row >= col
    outs = []
    ssq = jnp.zeros((tm, 1), F32)
    uz_pairs = [project(0, 4 * gd)]
    for g in range(N_SGU_GROUPS):
        uz = uz_pairs[g // 2][:, (g % 2) * 2 * gd:(g % 2 + 1) * 2 * gd]
        if g == 0:
            uz_pairs.append(project(4 * gd, 4 * gd))
        elif g == 1:
            emit_layouts(0, qn, qa, qb, project(2 * sgu_w, attn_w) * Q_SCALE)
        elif g == 2:
            emit_layouts(1, kn, ka, kb, project(2 * sgu_w + attn_w, attn_w))
        elif g == 3:
            emit_layouts(2, vn, va, vb, project(2 * sgu_w + 2 * attn_w, attn_w))
        u = jax.nn.gelu(uz[:, :gd])
        z = jax.nn.gelu(uz[:, gd:])
        zc = z - jnp.mean(z, axis=-1, keepdims=True)
        zn = zc * lax.rsqrt(jnp.mean(zc * zc, axis=-1, keepdims=True) + EPS)
        zn = (zn * lng_ref[...] + lnb_ref[...]).astype(BF16)
        wm = jnp.where(causal, ws_ref[g], 0.0).astype(BF16)
        mixed = [jnp.dot(wm, zn[c * SGU_CHUNK:(c + 1) * SGU_CHUNK], preferred_element_type=F32)
                 + bs_ref[g] for c in range(tm // SGU_CHUNK)]
        o = u * jnp.concatenate(mixed, axis=0)
        ssq = ssq + jnp.sum(o * o, axis=-1, keepdims=True)
        outs.append(o)
    scale = lax.rsqrt(ssq / sgu_w + EPS)
    for g in range(N_SGU_GROUPS):
        sgu_ref[:, g * gd:(g + 1) * gd] = (outs[g] * scale * gout_ref[:, g * gd:(g + 1) * gd]).astype(BF16)


def _proj_sgu(x2, g_pre, w_in, ln_g, ln_b, w_sp, b_sp, g_sgu, attn_w, sgu_w):
    n, d = x2.shape
    tm = PROJ_TILE
    per_span = SPAN // tm
    n_slabs = attn_w // LANES
    n_shape = jax.ShapeDtypeStruct((n_slabs, n, LANES), BF16)
    a_shape = jax.ShapeDtypeStruct((n_slabs, n // SPAN, 4, SPAN // 4, LANES), BF16)
    b_shape = jax.ShapeDtypeStruct((n_slabs, n // SPAN, 16, SPAN // 16, LANES), BF16)
    n_spec = pl.BlockSpec((n_slabs, tm, LANES), lambda i: (0, i, 0))
    a_spec = pl.BlockSpec((n_slabs, 1, 4, tm // 4, LANES),
                          lambda i: (0, i // per_span, 0, i % per_span, 0))
    b_spec = pl.BlockSpec((n_slabs, 1, 16, tm // 16, LANES),
                          lambda i: (0, i // per_span, 0, i % per_span, 0))
    return pl.pallas_call(
        functools.partial(_proj_sgu_kernel, attn_w=attn_w, sgu_w=sgu_w),
        grid=(n // tm,),
        in_specs=[
            pl.BlockSpec((tm, d), lambda i: (i, 0)),
            _const_spec(g_pre.shape), _const_spec(w_in.shape), _const_spec(ln_g.shape),
            _const_spec(ln_b.shape), _const_spec(w_sp.shape), _const_spec(b_sp.shape),
            _const_spec(g_sgu.shape),
        ],
        out_specs=[n_spec] * 3 + [a_spec] * 3 + [b_spec] * 3
        + [pl.BlockSpec((tm, sgu_w), lambda i: (i, 0))],
        out_shape=[n_shape] * 3 + [a_shape] * 3 + [b_shape] * 3
        + [jax.ShapeDtypeStruct((n, sgu_w), BF16)],
        scratch_shapes=[pltpu.VMEM((3, n_slabs, tm, LANES), F32),
                        pltpu.VMEM((3, n_slabs, 4, tm // 4, LANES), F32)],
        compiler_params=pltpu.CompilerParams(dimension_semantics=("arbitrary",),
                                             vmem_limit_bytes=VMEM_LIMIT),
        name="proj_sgu",
    )(x2, g_pre, w_in, ln_g, ln_b, w_sp, b_sp, g_sgu)


A_BLOCKS = SPAN // 4 // QBLK


def _attn_kernel(slopes_ref, qn, kn, vn, knp, vnp, qa, ka, va, kap, vap, qb, kb, vb, kbp, vbp,
                 o_ref, bias_ref, ob, lb):
    slab = pl.program_id(1)
    span = pl.program_id(2)
    lane = lax.broadcasted_iota(jnp.int32, (QBLK, LANES), 1)
    head0 = lane < HEAD_DIM
    first = span == 0

    @pl.when(first)
    def _start_of_sequence():
        i = lax.broadcasted_iota(jnp.int32, (WIN, WIN), 0)
        j = lax.broadcasted_iota(jnp.int32, (WIN, WIN), 1)
        steps = QBLK + (i & (QBLK - 1)) - j
        valid = (steps >= 0) & (steps <= QBLK)
        slope = jnp.where(i < QBLK, slopes_ref[HEADS_PER_SLAB * slab],
                          slopes_ref[HEADS_PER_SLAB * slab + 1])
        for b, d in enumerate(DILATIONS):
            bias = jnp.where(valid, -(slope * LOG2E) * (steps * d).astype(F32), NEG)
            bias_ref[b, 1] = bias
            bias_ref[b, 0] = jnp.where(j >= QBLK, bias, NEG)

    not_first = jnp.where(first, 0, 1)
    ones = jnp.ones((WIN, LANES), BF16)

    def window(prev_blk, cur_ref, idx, nb):
        if nb > 0:
            return cur_ref[idx + (slice((nb - 1) * QBLK, (nb + 1) * QBLK), slice(None))]
        prev = jnp.where(first, jnp.zeros_like(prev_blk), prev_blk)
        return jnp.concatenate([prev, cur_ref[idx + (slice(0, QBLK), slice(None))]], axis=0)

    def block(b, q, kw, vw, nb):
        zero = jnp.zeros_like(q)
        qq = jnp.concatenate([jnp.where(head0, q, zero), jnp.where(head0, zero, q)], axis=0)
        s = lax.dot_general(qq, kw, (((1,), (1,)), ((), ())), preferred_element_type=F32)
        s = s + bias_ref[b, not_first if nb == 0 else 1]
        m = jnp.max(s, axis=1, keepdims=True)
        p = jnp.exp2(s - m).astype(BF16)
        acc = jnp.dot(p, jnp.concatenate([vw, ones], axis=1), preferred_element_type=F32)
        o_t = jnp.where(head0, acc[0:QBLK, 0:LANES], acc[QBLK:WIN, 0:LANES])
        l_t = jnp.where(head0, acc[0:QBLK, LANES:], acc[QBLK:WIN, LANES:])
        m_t = jnp.where(head0, m[0:QBLK], m[QBLK:WIN])
        return o_t / l_t, m_t + jnp.log2(l_t)

    for nb in range(SPAN // QBLK):
        q = qn[0, nb * QBLK:(nb + 1) * QBLK, :]
        o_t, lse_t = block(0, q, window(knp[0], kn, (0,), nb), window(vnp[0], vn, (0,), nb), nb)
        ob[0, nb * QBLK:(nb + 1) * QBLK, :] = o_t
        lb[0, nb * QBLK:(nb + 1) * QBLK, :] = lse_t
    for r in range(4):
        for nb in range(A_BLOCKS):
            q = qa[0, 0, r, nb * QBLK:(nb + 1) * QBLK, :]
            o_t, lse_t = block(1, q, window(kap[0, 0, r], ka, (0, 0, r), nb),
                               window(vap[0, 0, r], va, (0, 0, r), nb), nb)
            rows = slice((r * A_BLOCKS + nb) * QBLK, (r * A_BLOCKS + nb + 1) * QBLK)
            ob[1, rows, :] = o_t
            lb[1, rows, :] = lse_t
    for r in range(16):
        o_t, lse_t = block(2, qb[0, 0, r], window(kbp[0, 0, r], kb, (0, 0, r), 0),
                           window(vbp[0, 0, r], vb, (0, 0, r), 0), 0)
        rows = pl.ds((r % 4) * (SPAN // 4) + r // 4, QBLK, stride=4)
        ob[2, rows, :] = o_t
        lb[2, rows, :] = lse_t

    for c in range(SPAN // WIN):
        rows = slice(c * WIN, (c + 1) * WIN)
        tokens = pl.ds(c // 2 + (c % 2) * (4 * WIN), WIN, stride=4)
        l0, l1, l2 = lb[0, tokens, :], lb[1, rows, :], lb[2, rows, :]
        mx = jnp.maximum(jnp.maximum(l0, l1), l2)
        w0, w1, w2 = jnp.exp2(l0 - mx), jnp.exp2(l1 - mx), jnp.exp2(l2 - mx)
        num = w0 * ob[0, tokens, :] + w1 * ob[1, rows, :] + w2 * ob[2, rows, :]
        o_ref[0, tokens, :] = num / (w0 + w1 + w2)


def _attention(qkv_layouts, slopes, batch, seq):
    qn, kn, vn, qa, ka, va, qb, kb, vb = qkv_layouts
    n_slabs, n, _ = qn.shape
    assert seq % SPAN == 0
    spans = seq // SPAN

    def cur(block_shape):
        zeros = (0,) * (len(block_shape) - 2)
        return pl.BlockSpec(block_shape, lambda b, s, t: (s, b * spans + t) + zeros)

    n_spec = cur((1, SPAN, LANES))
    a_spec = cur((1, 1, 4, SPAN // 4, LANES))
    b_spec = cur((1, 1, 16, QBLK, LANES))
    np_spec = pl.BlockSpec((1, QBLK, LANES),
                           lambda b, s, t: (s, jnp.maximum((b * spans + t) * (SPAN // QBLK) - 1, 0), 0))
    ap_spec = pl.BlockSpec((1, 1, 4, QBLK, LANES),
                           lambda b, s, t: (s, jnp.maximum(b * spans + t - 1, 0), 0, A_BLOCKS - 1, 0))
    bp_spec = pl.BlockSpec((1, 1, 16, QBLK, LANES),
                           lambda b, s, t: (s, jnp.maximum(b * spans + t - 1, 0), 0, 0, 0))
    return pl.pallas_call(
        _attn_kernel,
        grid=(batch, n_slabs, spans),
        in_specs=[pl.BlockSpec(memory_space=pltpu.SMEM),
                  n_spec, n_spec, n_spec, np_spec, np_spec,
                  a_spec, a_spec, a_spec, ap_spec, ap_spec,
                  b_spec, b_spec, b_spec, bp_spec, bp_spec],
        out_specs=pl.BlockSpec((1, SPAN, LANES), lambda b, s, t: (s, b * spans + t, 0)),
        out_shape=jax.ShapeDtypeStruct((n_slabs, n, LANES), F32),
        scratch_shapes=[
            pltpu.VMEM((len(DILATIONS), 2, WIN, WIN), F32),
            pltpu.VMEM((len(DILATIONS), SPAN, LANES), F32),
            pltpu.VMEM((len(DILATIONS), SPAN, LANES), F32),
        ],
        compiler_params=pltpu.CompilerParams(
            dimension_semantics=("arbitrary", "arbitrary", "arbitrary"),
            vmem_limit_bytes=VMEM_LIMIT),
        name="attn",
    )(slopes, qn, kn, vn, kn, vn, qa, ka, va, ka, va, qb, kb, vb, kb, vb)


def _tail_kernel(x_ref, attn_ref, sgu_ref, p_ref, ga_ref, wout_ref, gpm_ref, gpf_ref,
                 wgu_ref, wd_ref, gpo_ref, wpg_ref, bpg_ref, wpp_ref, o_ref):
    d_ff = wd_ref.shape[0]
    bounds = (0,) + TAIL_SPLITS + (x_ref.shape[0],)
    chains = [slice(lo, hi) for lo, hi in zip(bounds[:-1], bounds[1:])]

    def dot(a, w_ref):
        return jnp.dot(a, w_ref[...], preferred_element_type=F32)

    pe = [dot(p_ref[rows, :].astype(BF16), wpp_ref) for rows in chains]
    groups = []
    for rows in chains:
        attn = jnp.concatenate([attn_ref[s, rows, :] for s in range(attn_ref.shape[0])], axis=-1)
        an = _rms(attn, ga_ref[...]).astype(BF16)
        groups.append(jnp.concatenate([an, sgu_ref[rows, :]], axis=-1))
    mixed = [dot(g, wout_ref) for g in groups]
    h = [x_ref[rows, :] + _rms(m, gpm_ref[...]) for rows, m in zip(chains, mixed)]
    f = [_rms(hc, gpf_ref[...]).astype(BF16) for hc in h]
    gu = [dot(fc, wgu_ref) for fc in f]
    hid = [(jax.nn.silu(g[:, :d_ff]) * g[:, d_ff:]).astype(BF16) for g in gu]
    y = [dot(hc, wd_ref) for hc in hid]
    h = [hc + _rms(yc, gpo_ref[...]) for hc, yc in zip(h, y)]
    gate = [jax.nn.sigmoid(dot(hc.astype(BF16), wpg_ref) + bpg_ref[...]) for hc in h]
    for rows, hc, gc, pc in zip(chains, h, gate, pe):
        o_ref[rows, :] = hc + gc * pc


def _tail(x2, attn, sgu, p2, *params):
    n, d = x2.shape
    tm = TAIL_TILE

    def tile(a):
        return pl.BlockSpec((tm, a.shape[1]), lambda i: (i, 0))

    return pl.pallas_call(
        _tail_kernel,
        grid=(n // tm,),
        in_specs=[tile(x2), pl.BlockSpec((attn.shape[0], tm, LANES), lambda i: (0, i, 0)),
                  tile(sgu), tile(p2)] + [_const_spec(a.shape) for a in params],
        out_specs=pl.BlockSpec((tm, d), lambda i: (i, 0)),
        out_shape=jax.ShapeDtypeStruct((n, d), F32),
        compiler_params=pltpu.CompilerParams(dimension_semantics=("arbitrary",),
                                             vmem_limit_bytes=VMEM_LIMIT),
        name="tail",
    )(x2, attn, sgu, p2, *params)


def _group_major(w_in, attn_w, sgu_w):
    gd = sgu_w // N_SGU_GROUPS
    u0 = 3 * attn_w
    z0 = u0 + sgu_w
    cols = []
    for g in range(N_SGU_GROUPS):
        cols += [w_in[:, u0 + g * gd:u0 + (g + 1) * gd], w_in[:, z0 + g * gd:z0 + (g + 1) * gd]]
    return jnp.concatenate(cols + [w_in[:, :u0]], axis=1)


def kernel(x, p, ln_pre_mix, w_in, sgu_ln_g, sgu_ln_b, w_spatial, b_spatial, attn_out_norm,
           sgu_out_norm, w_out, ln_post_mix, ln_pre_ffn, w_gate_up, w_down, ln_post_ffn,
           w_pe_gate, b_pe_gate, w_pe_proj):
    batch, seq, d = x.shape
    depth = w_in.shape[0]
    attn_w = attn_out_norm.shape[1]
    sgu_w = sgu_out_norm.shape[1]
    n_heads = attn_w // HEAD_DIM
    slopes = 2.0 ** (-8.0 * (jnp.arange(n_heads, dtype=F32) + 1.0) / n_heads)

    def row(a):
        return a.reshape(1, -1)

    h = x.reshape(batch * seq, d)
    for i in range(depth):
        *qkv_layouts, sgu = _proj_sgu(
            h, row(ln_pre_mix[i]), _group_major(w_in[i], attn_w, sgu_w).astype(BF16),
            row(sgu_ln_g[i]), row(sgu_ln_b[i]),
            w_spatial[i], b_spatial[i][:, :, None], row(sgu_out_norm[i]), attn_w, sgu_w)
        attn = _attention(qkv_layouts, slopes, batch, seq)
        h = _tail(h, attn, sgu, p[i].reshape(batch * seq, -1),
                  row(attn_out_norm[i]), w_out[i].astype(BF16), row(ln_post_mix[i]),
                  row(ln_pre_ffn[i]), w_gate_up[i].astype(BF16), w_down[i].astype(BF16),
                  row(ln_post_ffn[i]), w_pe_gate[i].astype(BF16), row(b_pe_gate[i]),
                  w_pe_proj[i].astype(BF16))
    return h.reshape(batch, seq, d)
```

```python
import functools

import jax
import jax.numpy as jnp
from jax import lax
from jax.experimental import pallas as pl
from jax.experimental.pallas import tpu as pltpu

F32 = jnp.float32
BF16 = jnp.bfloat16

HEAD_DIM = 64
QBLK = 128
WIN = 2 * QBLK
DILATIONS = (1, 4, 16)
SPAN = QBLK * DILATIONS[-1]
SGU_CHUNK = 128
N_SGU_GROUPS = 4
EPS = 1e-6
NEG = -1e30
LOG2E = 1.4426950408889634
Q_SCALE = HEAD_DIM ** -0.5 * LOG2E

LANES = 128
BF16_SUBLANES = 16
HEADS_PER_SLAB = LANES // HEAD_DIM

PROJ_TILE = 512
TAIL_TILE = 512
TAIL_SPLITS = (256,)
VMEM_LIMIT = 56 * 1024 * 1024


def _rms(x, g):
    return x * lax.rsqrt(jnp.mean(x * x, axis=-1, keepdims=True) + EPS) * g


def _const_spec(shape):
    zeros = (0,) * len(shape)
    return pl.BlockSpec(shape, lambda *_: zeros, pipeline_mode=pl.Buffered(1))


def _proj_sgu_kernel(*refs, attn_w, sgu_w, n_cast):
    x_ref, g_ref, w_ref, lng_ref, lnb_ref, ws_ref, bs_ref, gout_ref = refs[:8]
    cast_in = refs[8:8 + n_cast]
    qn, kn, vn, qa, ka, va, qb, kb, vb, sgu_ref = refs[8 + n_cast:18 + n_cast]
    cast_out = refs[18 + n_cast:18 + 2 * n_cast]
    tok_tmp, res_tmp = refs[18 + 2 * n_cast:]
    for src, dst in zip(cast_in, cast_out):
        dst[...] = src[...].astype(dst.dtype)
    tm = x_ref.shape[0]
    gd = sgu_w // N_SGU_GROUPS
    a = _rms(x_ref[...], g_ref[...]).astype(BF16)

    def project(c0, width):
        return jnp.dot(a, w_ref[:, c0:c0 + width], preferred_element_type=F32)

    def emit_layouts(idx, n_ref, a_ref, b_ref, cols):
        for s in range(attn_w // LANES):
            slab = cols[:, s * LANES:(s + 1) * LANES]
            n_ref[s] = slab.astype(BF16)
            tok_tmp[idx, s] = slab
            for r in range(4):
                every4 = tok_tmp[idx, s, pl.ds(r, tm // 4, stride=4), :]
                a_ref[s, 0, r] = every4.astype(BF16)
                res_tmp[idx, s, r] = every4
                for r2 in range(4):
                    b_ref[s, 0, 4 * r2 + r] = (
                        res_tmp[idx, s, r, pl.ds(r2, tm // 16, stride=4), :].astype(BF16))

    row = lax.broadcasted_iota(jnp.int32, (SGU_CHUNK, SGU_CHUNK), 0)
    col = lax.broadcasted_iota(jnp.int32, (SGU_CHUNK, SGU_CHUNK), 1)
    causal = row >= col
    outs = []
    ssq = jnp.zeros((tm, 1), F32)
    uz_pairs = [project(0, 4 * gd)]
    for g in range(N_SGU_GROUPS):
        uz = uz_pairs[g // 2][:, (g % 2) * 2 * gd:(g % 2 + 1) * 2 * gd]
        if g == 0:
            uz_pairs.append(project(4 * gd, 4 * gd))
        elif g == 1:
            emit_layouts(0, qn, qa, qb, project(2 * sgu_w, attn_w) * Q_SCALE)
        elif g == 2:
            emit_layouts(1, kn, ka, kb, project(2 * sgu_w + attn_w, attn_w))
        elif g == 3:
            emit_layouts(2, vn, va, vb, project(2 * sgu_w + 2 * attn_w, attn_w))
        u = jax.nn.gelu(uz[:, :gd])
        z = jax.nn.gelu(uz[:, gd:])
        zc = z - jnp.mean(z, axis=-1, keepdims=True)
        zn = zc * lax.rsqrt(jnp.mean(zc * zc, axis=-1, keepdims=True) + EPS)
        zn = (zn * lng_ref[...] + lnb_ref[...]).astype(BF16)
        wm = jnp.where(causal, ws_ref[g], 0.0).astype(BF16)
        mixed = [jnp.dot(wm, zn[c * SGU_CHUNK:(c + 1) * SGU_CHUNK], preferred_element_type=F32)
                 + bs_ref[g] for c in range(tm // SGU_CHUNK)]
        o = u * jnp.concatenate(mixed, axis=0)
        ssq = ssq + jnp.sum(o * o, axis=-1, keepdims=True)
        outs.append(o)
    scale = lax.rsqrt(ssq / sgu_w + EPS)
    for g in range(N_SGU_GROUPS):
        sgu_ref[:, g * gd:(g + 1) * gd] = (outs[g] * scale * gout_ref[:, g * gd:(g + 1) * gd]).astype(BF16)


def _cast_chunk_spec(w, steps):
    rows = w.shape[0]
    chunk = max(BF16_SUBLANES, -(-rows // steps // BF16_SUBLANES) * BF16_SUBLANES)
    while rows % chunk:
        chunk += BF16_SUBLANES
    last = rows // chunk - 1
    return pl.BlockSpec((chunk, w.shape[1]), lambda i: (jnp.minimum(i, last), 0))


def _proj_sgu(x2, g_pre, w_in, ln_g, ln_b, w_sp, b_sp, g_sgu, attn_w, sgu_w, to_cast):
    n, d = x2.shape
    tm = PROJ_TILE
    per_span = SPAN // tm
    cast_specs = [_cast_chunk_spec(w, n // tm) for w in to_cast]
    n_slabs = attn_w // LANES
    n_shape = jax.ShapeDtypeStruct((n_slabs, n, LANES), BF16)
    a_shape = jax.ShapeDtypeStruct((n_slabs, n // SPAN, 4, SPAN // 4, LANES), BF16)
    b_shape = jax.ShapeDtypeStruct((n_slabs, n // SPAN, 16, SPAN // 16, LANES), BF16)
    n_spec = pl.BlockSpec((n_slabs, tm, LANES), lambda i: (0, i, 0))
    a_spec = pl.BlockSpec((n_slabs, 1, 4, tm // 4, LANES),
                          lambda i: (0, i // per_span, 0, i % per_span, 0))
    b_spec = pl.BlockSpec((n_slabs, 1, 16, tm // 16, LANES),
                          lambda i: (0, i // per_span, 0, i % per_span, 0))
    return pl.pallas_call(
        functools.partial(_proj_sgu_kernel, attn_w=attn_w, sgu_w=sgu_w, n_cast=len(to_cast)),
        grid=(n // tm,),
        in_specs=[
            pl.BlockSpec((tm, d), lambda i: (i, 0)),
            _const_spec(g_pre.shape), _const_spec(w_in.shape), _const_spec(ln_g.shape),
            _const_spec(ln_b.shape), _const_spec(w_sp.shape), _const_spec(b_sp.shape),
            _const_spec(g_sgu.shape),
        ] + cast_specs,
        out_specs=[n_spec] * 3 + [a_spec] * 3 + [b_spec] * 3
        + [pl.BlockSpec((tm, sgu_w), lambda i: (i, 0))] + cast_specs,
        out_shape=[n_shape] * 3 + [a_shape] * 3 + [b_shape] * 3
        + [jax.ShapeDtypeStruct((n, sgu_w), BF16)]
        + [jax.ShapeDtypeStruct(w.shape, BF16) for w in to_cast],
        scratch_shapes=[pltpu.VMEM((3, n_slabs, tm, LANES), F32),
                        pltpu.VMEM((3, n_slabs, 4, tm // 4, LANES), F32)],
        compiler_params=pltpu.CompilerParams(dimension_semantics=("arbitrary",),
                                             vmem_limit_bytes=VMEM_LIMIT),
        name="proj_sgu",
    )(x2, g_pre, w_in, ln_g, ln_b, w_sp, b_sp, g_sgu, *to_cast)


A_BLOCKS = SPAN // 4 // QBLK


def _attn_kernel(slopes_ref, qn, kn, vn, knp, vnp, qa, ka, va, kap, vap, qb, kb, vb, kbp, vbp,
                 o_ref, bias_ref, ob, lb):
    slab = pl.program_id(1)
    span = pl.program_id(2)
    lane = lax.broadcasted_iota(jnp.int32, (QBLK, LANES), 1)
    head0 = lane < HEAD_DIM
    first = span == 0

    @pl.when(first)
    def _start_of_sequence():
        i = lax.broadcasted_iota(jnp.int32, (WIN, WIN), 0)
        j = lax.broadcasted_iota(jnp.int32, (WIN, WIN), 1)
        steps = QBLK + (i & (QBLK - 1)) - j
        valid = (steps >= 0) & (steps <= QBLK)
        slope = jnp.where(i < QBLK, slopes_ref[HEADS_PER_SLAB * slab],
                          slopes_ref[HEADS_PER_SLAB * slab + 1])
        for b, d in enumerate(DILATIONS):
            bias = jnp.where(valid, -(slope * LOG2E) * (steps * d).astype(F32), NEG)
            bias_ref[b, 1] = bias
            bias_ref[b, 0] = jnp.where(j >= QBLK, bias, NEG)

    not_first = jnp.where(first, 0, 1)
    ones = jnp.ones((WIN, LANES), BF16)

    def window(prev_blk, cur_ref, idx, nb):
        if nb > 0:
            return cur_ref[idx + (slice((nb - 1) * QBLK, (nb + 1) * QBLK), slice(None))]
        prev = jnp.where(first, jnp.zeros_like(prev_blk), prev_blk)
        return jnp.concatenate([prev, cur_ref[idx + (slice(0, QBLK), slice(None))]], axis=0)

    def block(b, q, kw, vw, nb):
        zero = jnp.zeros_like(q)
        qq = jnp.concatenate([jnp.where(head0, q, zero), jnp.where(head0, zero, q)], axis=0)
        s = lax.dot_general(qq, kw, (((1,), (1,)), ((), ())), preferred_element_type=F32)
        s = s + bias_ref[b, not_first if nb == 0 else 1]
        m = jnp.max(s, axis=1, keepdims=True)
        p = jnp.exp2(s - m).astype(BF16)
        acc = jnp.dot(p, jnp.concatenate([vw, ones], axis=1), preferred_element_type=F32)
        o_t = jnp.where(head0, acc[0:QBLK, 0:LANES], acc[QBLK:WIN, 0:LANES])
        l_t = jnp.where(head0, acc[0:QBLK, LANES:], acc[QBLK:WIN, LANES:])
        m_t = jnp.where(head0, m[0:QBLK], m[QBLK:WIN])
        return o_t / l_t, m_t + jnp.log2(l_t)

    for nb in range(SPAN // QBLK):
        q = qn[0, nb * QBLK:(nb + 1) * QBLK, :]
        o_t, lse_t = block(0, q, window(knp[0], kn, (0,), nb), window(vnp[0], vn, (0,), nb), nb)
        ob[0, nb * QBLK:(nb + 1) * QBLK, :] = o_t
        lb[0, nb * QBLK:(nb + 1) * QBLK, :] = lse_t
    for r in range(4):
        for nb in range(A_BLOCKS):
            q = qa[0, 0, r, nb * QBLK:(nb + 1) * QBLK, :]
            o_t, lse_t = block(1, q, window(kap[0, 0, r], ka, (0, 0, r), nb),
                               window(vap[0, 0, r], va, (0, 0, r), nb), nb)
            rows = slice((r * A_BLOCKS + nb) * QBLK, (r * A_BLOCKS + nb + 1) * QBLK)
            ob[1, rows, :] = o_t
            lb[1, rows, :] = lse_t
    for r in range(16):
        o_t, lse_t = block(2, qb[0, 0, r], window(kbp[0, 0, r], kb, (0, 0, r), 0),
                           window(vbp[0, 0, r], vb, (0, 0, r), 0), 0)
        rows = pl.ds((r % 4) * (SPAN // 4) + r // 4, QBLK, stride=4)
        ob[2, rows, :] = o_t
        lb[2, rows, :] = lse_t

    for c in range(SPAN // WIN):
        rows = slice(c * WIN, (c + 1) * WIN)
        tokens = pl.ds(c // 2 + (c % 2) * (4 * WIN), WIN, stride=4)
        l0, l1, l2 = lb[0, tokens, :], lb[1, rows, :], lb[2, rows, :]
        mx = jnp.maximum(jnp.maximum(l0, l1), l2)
        w0, w1, w2 = jnp.exp2(l0 - mx), jnp.exp2(l1 - mx), jnp.exp2(l2 - mx)
        num = w0 * ob[0, tokens, :] + w1 * ob[1, rows, :] + w2 * ob[2, rows, :]
        o_ref[0, tokens, :] = num / (w0 + w1 + w2)


def _attention(qkv_layouts, slopes, batch, seq):
    qn, kn, vn, qa, ka, va, qb, kb, vb = qkv_layouts
    n_slabs, n, _ = qn.shape
    assert seq % SPAN == 0
    spans = seq // SPAN

    def cur(block_shape):
        zeros = (0,) * (len(block_shape) - 2)
        return pl.BlockSpec(block_shape, lambda b, s, t: (s, b * spans + t) + zeros)

    n_spec = cur((1, SPAN, LANES))
    a_spec = cur((1, 1, 4, SPAN // 4, LANES))
    b_spec = cur((1, 1, 16, QBLK, LANES))
    np_spec = pl.BlockSpec((1, QBLK, LANES),
                           lambda b, s, t: (s, jnp.maximum((b * spans + t) * (SPAN // QBLK) - 1, 0), 0))
    ap_spec = pl.BlockSpec((1, 1, 4, QBLK, LANES),
                           lambda b, s, t: (s, jnp.maximum(b * spans + t - 1, 0), 0, A_BLOCKS - 1, 0))
    bp_spec = pl.BlockSpec((1, 1, 16, QBLK, LANES),
                           lambda b, s, t: (s, jnp.maximum(b * spans + t - 1, 0), 0, 0, 0))
    return pl.pallas_call(
        _attn_kernel,
        grid=(batch, n_slabs, spans),
        in_specs=[pl.BlockSpec(memory_space=pltpu.SMEM),
                  n_spec, n_spec, n_spec, np_spec, np_spec,
                  a_spec, a_spec, a_spec, ap_spec, ap_spec,
                  b_spec, b_spec, b_spec, bp_spec, bp_spec],
        out_specs=pl.BlockSpec((1, SPAN, LANES), lambda b, s, t: (s, b * spans + t, 0)),
        out_shape=jax.ShapeDtypeStruct((n_slabs, n, LANES), F32),
        scratch_shapes=[
            pltpu.VMEM((len(DILATIONS), 2, WIN, WIN), F32),
            pltpu.VMEM((len(DILATIONS), SPAN, LANES), F32),
            pltpu.VMEM((len(DILATIONS), SPAN, LANES), F32),
        ],
        compiler_params=pltpu.CompilerParams(
            dimension_semantics=("arbitrary", "arbitrary", "arbitrary"),
            vmem_limit_bytes=VMEM_LIMIT),
        name="attn",
    )(slopes, qn, kn, vn, kn, vn, qa, ka, va, ka, va, qb, kb, vb, kb, vb)


def _tail_kernel(x_ref, attn_ref, sgu_ref, p_ref, ga_ref, wout_ref, gpm_ref, gpf_ref,
                 wgu_ref, wd_ref, gpo_ref, wpg_ref, bpg_ref, wpp_ref, o_ref):
    d_ff = wd_ref.shape[0]
    bounds = (0,) + TAIL_SPLITS + (x_ref.shape[0],)
    chains = [slice(lo, hi) for lo, hi in zip(bounds[:-1], bounds[1:])]

    def dot(a, w_ref):
        return jnp.dot(a, w_ref[...], preferred_element_type=F32)

    pe = [dot(p_ref[rows, :].astype(BF16), wpp_ref) for rows in chains]
    groups = []
    for rows in chains:
        attn = jnp.concatenate([attn_ref[s, rows, :] for s in range(attn_ref.shape[0])], axis=-1)
        an = _rms(attn, ga_ref[...]).astype(BF16)
        groups.append(jnp.concatenate([an, sgu_ref[rows, :]], axis=-1))
    mixed = [dot(g, wout_ref) for g in groups]
    h = [x_ref[rows, :] + _rms(m, gpm_ref[...]) for rows, m in zip(chains, mixed)]
    f = [_rms(hc, gpf_ref[...]).astype(BF16) for hc in h]
    gu = [dot(fc, wgu_ref) for fc in f]
    hid = [(jax.nn.silu(g[:, :d_ff]) * g[:, d_ff:]).astype(BF16) for g in gu]
    y = [dot(hc, wd_ref) for hc in hid]
    h = [hc + _rms(yc, gpo_ref[...]) for hc, yc in zip(h, y)]
    gate = [jax.nn.sigmoid(dot(hc.astype(BF16), wpg_ref) + bpg_ref[...]) for hc in h]
    for rows, hc, gc, pc in zip(chains, h, gate, pe):
        o_ref[rows, :] = hc + gc * pc


def _tail(x2, attn, sgu, p2, *params):
    n, d = x2.shape
    tm = TAIL_TILE

    def tile(a):
        return pl.BlockSpec((tm, a.shape[1]), lambda i: (i, 0))

    return pl.pallas_call(
        _tail_kernel,
        grid=(n // tm,),
        in_specs=[tile(x2), pl.BlockSpec((attn.shape[0], tm, LANES), lambda i: (0, i, 0)),
                  tile(sgu), tile(p2)] + [_const_spec(a.shape) for a in params],
        out_specs=pl.BlockSpec((tm, d), lambda i: (i, 0)),
        out_shape=jax.ShapeDtypeStruct((n, d), F32),
        compiler_params=pltpu.CompilerParams(dimension_semantics=("arbitrary",),
                                             vmem_limit_bytes=VMEM_LIMIT),
        name="tail",
    )(x2, attn, sgu, p2, *params)


def _group_major(w_in, attn_w, sgu_w):
    gd = sgu_w // N_SGU_GROUPS
    u0 = 3 * attn_w
    z0 = u0 + sgu_w
    cols = []
    for g in range(N_SGU_GROUPS):
        cols += [w_in[:, u0 + g * gd:u0 + (g + 1) * gd], w_in[:, z0 + g * gd:z0 + (g + 1) * gd]]
    return jnp.concatenate(cols + [w_in[:, :u0]], axis=1)


def kernel(x, p, ln_pre_mix, w_in, sgu_ln_g, sgu_ln_b, w_spatial, b_spatial, attn_out_norm,
           sgu_out_norm, w_out, ln_post_mix, ln_pre_ffn, w_gate_up, w_down, ln_post_ffn,
           w_pe_gate, b_pe_gate, w_pe_proj):
    batch, seq, d = x.shape
    depth = w_in.shape[0]
    attn_w = attn_out_norm.shape[1]
    sgu_w = sgu_out_norm.shape[1]
    n_heads = attn_w // HEAD_DIM
    slopes = 2.0 ** (-8.0 * (jnp.arange(n_heads, dtype=F32) + 1.0) / n_heads)

    def row(a):
        return a.reshape(1, -1)

    h = x.reshape(batch * seq, d)
    for i in range(depth):
        tail_weights = (w_out[i], w_gate_up[i], w_down[i], w_pe_gate[i], w_pe_proj[i])
        outs = _proj_sgu(
            h, row(ln_pre_mix[i]), _group_major(w_in[i], attn_w, sgu_w).astype(BF16),
            row(sgu_ln_g[i]), row(sgu_ln_b[i]),
            w_spatial[i], b_spatial[i][:, :, None], row(sgu_out_norm[i]), attn_w, sgu_w,
            tail_weights)
        qkv_layouts, sgu = outs[:9], outs[9]
        wout_b, wgu_b, wd_b, wpg_b, wpp_b = outs[10:]
        attn = _attention(qkv_layouts, slopes, batch, seq)
        h = _tail(h, attn, sgu, p[i].reshape(batch * seq, -1),
                  row(attn_out_norm[i]), wout_b, row(ln_post_mix[i]),
                  row(ln_pre_ffn[i]), wgu_b, wd_b,
                  row(ln_post_ffn[i]), wpg_b, row(b_pe_gate[i]), wpp_b)
    return h.reshape(batch, seq, d)
```

```python
import functools

import jax
import jax.numpy as jnp
from jax import lax
from jax.experimental import pallas as pl
from jax.experimental.pallas import tpu as pltpu

F32 = jnp.float32
BF16 = jnp.bfloat16

HEAD_DIM = 64
QBLK = 128
WIN = 2 * QBLK
DILATIONS = (1, 4, 16)
SPAN = QBLK * DILATIONS[-1]
SGU_CHUNK = 128
N_SGU_GROUPS = 4
EPS = 1e-6
NEG = -1e30
LOG2E = 1.4426950408889634
Q_SCALE = HEAD_DIM ** -0.5 * LOG2E

LANES = 128
BF16_SUBLANES = 16
HEADS_PER_SLAB = LANES // HEAD_DIM

PROJ_TILE = 512
TAIL_TILE = 512
TAIL_SPLITS = (256,)
VMEM_LIMIT = 56 * 1024 * 1024


def _unit_rms(x):
    return x * lax.rsqrt(jnp.mean(x * x, axis=-1, keepdims=True) + EPS)


def _rms(x, g):
    return _unit_rms(x) * g


def _const_spec(shape):
    zeros = (0,) * len(shape)
    return pl.BlockSpec(shape, lambda *_: zeros, pipeline_mode=pl.Buffered(1))


def _proj_sgu_kernel(*refs, attn_w, sgu_w, n_cast):
    x_ref, w_ref, lng_ref, lnb_ref, ws_ref, bs_ref = refs[:6]
    cast_in = refs[6:6 + 2 * n_cast]
    qn, kn, vn, qa, ka, va, qb, kb, vb, sgu_ref = refs[6 + 2 * n_cast:16 + 2 * n_cast]
    cast_out = refs[16 + 2 * n_cast:16 + 3 * n_cast]
    tok_tmp, res_tmp = refs[16 + 3 * n_cast:]
    for src, row_scale, dst in zip(cast_in[0::2], cast_in[1::2], cast_out):
        dst[...] = (src[...] * row_scale[...]).astype(dst.dtype)
    tm = x_ref.shape[0]
    gd = sgu_w // N_SGU_GROUPS
    a = _unit_rms(x_ref[...]).astype(BF16)

    def project(c0, width):
        return jnp.dot(a, w_ref[:, c0:c0 + width], preferred_element_type=F32)

    def emit_layouts(idx, n_ref, a_ref, b_ref, cols):
        for s in range(attn_w // LANES):
            slab = cols[:, s * LANES:(s + 1) * LANES]
            n_ref[s] = slab.astype(BF16)
            tok_tmp[idx, s] = slab
            for r in range(4):
                every4 = tok_tmp[idx, s, pl.ds(r, tm // 4, stride=4), :]
                a_ref[s, 0, r] = every4.astype(BF16)
                res_tmp[idx, s, r] = every4
                for r2 in range(4):
                    b_ref[s, 0, 4 * r2 + r] = (
                        res_tmp[idx, s, r, pl.ds(r2, tm // 16, stride=4), :].astype(BF16))

    row = lax.broadcasted_iota(jnp.int32, (SGU_CHUNK, SGU_CHUNK), 0)
    col = lax.broadcasted_iota(jnp.int32, (SGU_CHUNK, SGU_CHUNK), 1)
    causal = row >= col
    outs = []
    ssq = jnp.zeros((tm, 1), F32)
    uz_pairs = [project(0, 4 * gd)]
    for g in range(N_SGU_GROUPS):
        uz = uz_pairs[g // 2][:, (g % 2) * 2 * gd:(g % 2 + 1) * 2 * gd]
        if g == 0:
            uz_pairs.append(project(4 * gd, 4 * gd))
        elif g == 1:
            emit_layouts(0, qn, qa, qb, project(2 * sgu_w, attn_w) * Q_SCALE)
        elif g == 2:
            emit_layouts(1, kn, ka, kb, project(2 * sgu_w + attn_w, attn_w))
        elif g == 3:
            emit_layouts(2, vn, va, vb, project(2 * sgu_w + 2 * attn_w, attn_w))
        u = jax.nn.gelu(uz[:, :gd])
        z = jax.nn.gelu(uz[:, gd:])
        zc = z - jnp.mean(z, axis=-1, keepdims=True)
        zn = zc * lax.rsqrt(jnp.mean(zc * zc, axis=-1, keepdims=True) + EPS)
        zn = (zn * lng_ref[...] + lnb_ref[...]).astype(BF16)
        wm = jnp.where(causal, ws_ref[g], 0.0).astype(BF16)
        mixed = [jnp.dot(wm, zn[c * SGU_CHUNK:(c + 1) * SGU_CHUNK], preferred_element_type=F32)
                 + bs_ref[g] for c in range(tm // SGU_CHUNK)]
        o = u * jnp.concatenate(mixed, axis=0)
        ssq = ssq + jnp.sum(o * o, axis=-1, keepdims=True)
        outs.append(o)
    scale = lax.rsqrt(ssq / sgu_w + EPS)
    for g in range(N_SGU_GROUPS):
        sgu_ref[:, g * gd:(g + 1) * gd] = (outs[g] * scale).astype(BF16)


def _cast_chunk_specs(w, steps):
    rows = w.shape[0]
    chunk = max(BF16_SUBLANES, -(-rows // steps // BF16_SUBLANES) * BF16_SUBLANES)
    while rows % chunk:
        chunk += BF16_SUBLANES
    last = rows // chunk - 1
    return (pl.BlockSpec((chunk, w.shape[1]), lambda i: (jnp.minimum(i, last), 0)),
            pl.BlockSpec((chunk, 1), lambda i: (jnp.minimum(i, last), 0)))


def _proj_sgu(x2, w_in, ln_g, ln_b, w_sp, b_sp, attn_w, sgu_w, to_cast):
    n, d = x2.shape
    tm = PROJ_TILE
    per_span = SPAN // tm
    cast_specs = [_cast_chunk_specs(w, n // tm) for w, _ in to_cast]
    n_slabs = attn_w // LANES
    n_shape = jax.ShapeDtypeStruct((n_slabs, n, LANES), BF16)
    a_shape = jax.ShapeDtypeStruct((n_slabs, n // SPAN, 4, SPAN // 4, LANES), BF16)
    b_shape = jax.ShapeDtypeStruct((n_slabs, n // SPAN, 16, SPAN // 16, LANES), BF16)
    n_spec = pl.BlockSpec((n_slabs, tm, LANES), lambda i: (0, i, 0))
    a_spec = pl.BlockSpec((n_slabs, 1, 4, tm // 4, LANES),
                          lambda i: (0, i // per_span, 0, i % per_span, 0))
    b_spec = pl.BlockSpec((n_slabs, 1, 16, tm // 16, LANES),
                          lambda i: (0, i // per_span, 0, i % per_span, 0))
    return pl.pallas_call(
        functools.partial(_proj_sgu_kernel, attn_w=attn_w, sgu_w=sgu_w, n_cast=len(to_cast)),
        grid=(n // tm,),
        in_specs=[
            pl.BlockSpec((tm, d), lambda i: (i, 0)),
            _const_spec(w_in.shape), _const_spec(ln_g.shape),
            _const_spec(ln_b.shape), _const_spec(w_sp.shape), _const_spec(b_sp.shape),
        ] + [spec for pair in cast_specs for spec in pair],
        out_specs=[n_spec] * 3 + [a_spec] * 3 + [b_spec] * 3
        + [pl.BlockSpec((tm, sgu_w), lambda i: (i, 0))] + [pair[0] for pair in cast_specs],
        out_shape=[n_shape] * 3 + [a_shape] * 3 + [b_shape] * 3
        + [jax.ShapeDtypeStruct((n, sgu_w), BF16)]
        + [jax.ShapeDtypeStruct(w.shape, BF16) for w, _ in to_cast],
        scratch_shapes=[pltpu.VMEM((3, n_slabs, tm, LANES), F32),
                        pltpu.VMEM((3, n_slabs, 4, tm // 4, LANES), F32)],
        compiler_params=pltpu.CompilerParams(dimension_semantics=("arbitrary",),
                                             vmem_limit_bytes=VMEM_LIMIT),
        name="proj_sgu",
    )(x2, w_in, ln_g, ln_b, w_sp, b_sp, *[a for pair in to_cast for a in pair])


A_BLOCKS = SPAN // 4 // QBLK


def _attn_kernel(slopes_ref, qn, kn, vn, knp, vnp, qa, ka, va, kap, vap, qb, kb, vb, kbp, vbp,
                 o_ref, bias_ref, ob, lb):
    slab = pl.program_id(1)
    span = pl.program_id(2)
    lane = lax.broadcasted_iota(jnp.int32, (QBLK, LANES), 1)
    head0 = lane < HEAD_DIM
    first = span == 0

    @pl.when(first)
    def _start_of_sequence():
        i = lax.broadcasted_iota(jnp.int32, (WIN, WIN), 0)
        j = lax.broadcasted_iota(jnp.int32, (WIN, WIN), 1)
        steps = QBLK + (i & (QBLK - 1)) - j
        valid = (steps >= 0) & (steps <= QBLK)
        slope = jnp.where(i < QBLK, slopes_ref[HEADS_PER_SLAB * slab],
                          slopes_ref[HEADS_PER_SLAB * slab + 1])
        for b, d in enumerate(DILATIONS):
            bias = jnp.where(valid, -(slope * LOG2E) * (steps * d).astype(F32), NEG)
            bias_ref[b, 1] = bias
            bias_ref[b, 0] = jnp.where(j >= QBLK, bias, NEG)

    not_first = jnp.where(first, 0, 1)
    ones = jnp.ones((WIN, LANES), BF16)

    def window(prev_blk, cur_ref, idx, nb):
        if nb > 0:
            return cur_ref[idx + (slice((nb - 1) * QBLK, (nb + 1) * QBLK), slice(None))]
        prev = jnp.where(first, jnp.zeros_like(prev_blk), prev_blk)
        return jnp.concatenate([prev, cur_ref[idx + (slice(0, QBLK), slice(None))]], axis=0)

    def block(b, q, kw, vw, nb):
        zero = jnp.zeros_like(q)
        qq = jnp.concatenate([jnp.where(head0, q, zero), jnp.where(head0, zero, q)], axis=0)
        s = lax.dot_general(qq, kw, (((1,), (1,)), ((), ())), preferred_element_type=F32)
        s = s + bias_ref[b, not_first if nb == 0 else 1]
        m = jnp.max(s, axis=1, keepdims=True)
        p = jnp.exp2(s - m).astype(BF16)
        acc = jnp.dot(p, jnp.concatenate([vw, ones], axis=1), preferred_element_type=F32)
        o_t = jnp.where(head0, acc[0:QBLK, 0:LANES], acc[QBLK:WIN, 0:LANES])
        l_t = jnp.where(head0, acc[0:QBLK, LANES:], acc[QBLK:WIN, LANES:])
        m_t = jnp.where(head0, m[0:QBLK], m[QBLK:WIN])
        return o_t / l_t, m_t + jnp.log2(l_t)

    for nb in range(SPAN // QBLK):
        q = qn[0, nb * QBLK:(nb + 1) * QBLK, :]
        o_t, lse_t = block(0, q, window(knp[0], kn, (0,), nb), window(vnp[0], vn, (0,), nb), nb)
        ob[0, nb * QBLK:(nb + 1) * QBLK, :] = o_t
        lb[0, nb * QBLK:(nb + 1) * QBLK, :] = lse_t
    for r in range(4):
        for nb in range(A_BLOCKS):
            q = qa[0, 0, r, nb * QBLK:(nb + 1) * QBLK, :]
            o_t, lse_t = block(1, q, window(kap[0, 0, r], ka, (0, 0, r), nb),
                               window(vap[0, 0, r], va, (0, 0, r), nb), nb)
            rows = slice((r * A_BLOCKS + nb) * QBLK, (r * A_BLOCKS + nb + 1) * QBLK)
            ob[1, rows, :] = o_t
            lb[1, rows, :] = lse_t
    for r in range(16):
        o_t, lse_t = block(2, qb[0, 0, r], window(kbp[0, 0, r], kb, (0, 0, r), 0),
                           window(vbp[0, 0, r], vb, (0, 0, r), 0), 0)
        rows = pl.ds((r % 4) * (SPAN // 4) + r // 4, QBLK, stride=4)
        ob[2, rows, :] = o_t
        lb[2, rows, :] = lse_t

    for c in range(SPAN // WIN):
        rows = slice(c * WIN, (c + 1) * WIN)
        tokens = pl.ds(c // 2 + (c % 2) * (4 * WIN), WIN, stride=4)
        l0, l1, l2 = lb[0, tokens, :], lb[1, rows, :], lb[2, rows, :]
        mx = jnp.maximum(jnp.maximum(l0, l1), l2)
        w0, w1, w2 = jnp.exp2(l0 - mx), jnp.exp2(l1 - mx), jnp.exp2(l2 - mx)
        num = w0 * ob[0, tokens, :] + w1 * ob[1, rows, :] + w2 * ob[2, rows, :]
        o_ref[0, tokens, :] = num / (w0 + w1 + w2)


def _attention(qkv_layouts, slopes, batch, seq):
    qn, kn, vn, qa, ka, va, qb, kb, vb = qkv_layouts
    n_slabs, n, _ = qn.shape
    assert seq % SPAN == 0
    spans = seq // SPAN

    def cur(block_shape):
        zeros = (0,) * (len(block_shape) - 2)
        return pl.BlockSpec(block_shape, lambda b, s, t: (s, b * spans + t) + zeros)

    n_spec = cur((1, SPAN, LANES))
    a_spec = cur((1, 1, 4, SPAN // 4, LANES))
    b_spec = cur((1, 1, 16, QBLK, LANES))
    np_spec = pl.BlockSpec((1, QBLK, LANES),
                           lambda b, s, t: (s, jnp.maximum((b * spans + t) * (SPAN // QBLK) - 1, 0), 0))
    ap_spec = pl.BlockSpec((1, 1, 4, QBLK, LANES),
                           lambda b, s, t: (s, jnp.maximum(b * spans + t - 1, 0), 0, A_BLOCKS - 1, 0))
    bp_spec = pl.BlockSpec((1, 1, 16, QBLK, LANES),
                           lambda b, s, t: (s, jnp.maximum(b * spans + t - 1, 0), 0, 0, 0))
    return pl.pallas_call(
        _attn_kernel,
        grid=(batch, n_slabs, spans),
        in_specs=[pl.BlockSpec(memory_space=pltpu.SMEM),
                  n_spec, n_spec, n_spec, np_spec, np_spec,
                  a_spec, a_spec, a_spec, ap_spec, ap_spec,
                  b_spec, b_spec, b_spec, bp_spec, bp_spec],
        out_specs=pl.BlockSpec((1, SPAN, LANES), lambda b, s, t: (s, b * spans + t, 0)),
        out_shape=jax.ShapeDtypeStruct((n_slabs, n, LANES), F32),
        scratch_shapes=[
            pltpu.VMEM((len(DILATIONS), 2, WIN, WIN), F32),
            pltpu.VMEM((len(DILATIONS), SPAN, LANES), F32),
            pltpu.VMEM((len(DILATIONS), SPAN, LANES), F32),
        ],
        compiler_params=pltpu.CompilerParams(
            dimension_semantics=("arbitrary", "arbitrary", "arbitrary"),
            vmem_limit_bytes=VMEM_LIMIT),
        name="attn",
    )(slopes, qn, kn, vn, kn, vn, qa, ka, va, ka, va, qb, kb, vb, kb, vb)


def _tail_kernel(x_ref, attn_ref, sgu_ref, p_ref, wout_ref, gpm_ref,
                 wgu_ref, wd_ref, gpo_ref, wpg_ref, bpg_ref, wpp_ref, o_ref):
    d_ff = wd_ref.shape[0]
    bounds = (0,) + TAIL_SPLITS + (x_ref.shape[0],)
    chains = [slice(lo, hi) for lo, hi in zip(bounds[:-1], bounds[1:])]

    def dot(a, w_ref):
        return jnp.dot(a, w_ref[...], preferred_element_type=F32)

    pe = [dot(p_ref[rows, :].astype(BF16), wpp_ref) for rows in chains]
    groups = []
    for rows in chains:
        attn = jnp.concatenate([attn_ref[s, rows, :] for s in range(attn_ref.shape[0])], axis=-1)
        an = _unit_rms(attn).astype(BF16)
        groups.append(jnp.concatenate([an, sgu_ref[rows, :]], axis=-1))
    mixed = [dot(g, wout_ref) for g in groups]
    h = [x_ref[rows, :] + _rms(m, gpm_ref[...]) for rows, m in zip(chains, mixed)]
    f = [_unit_rms(hc).astype(BF16) for hc in h]
    gu = [dot(fc, wgu_ref) for fc in f]
    hid = [(jax.nn.silu(g[:, :d_ff]) * g[:, d_ff:]).astype(BF16) for g in gu]
    y = [dot(hc, wd_ref) for hc in hid]
    h = [hc + _rms(yc, gpo_ref[...]) for hc, yc in zip(h, y)]
    gate = [jax.nn.sigmoid(dot(hc.astype(BF16), wpg_ref) + bpg_ref[...]) for hc in h]
    for rows, hc, gc, pc in zip(chains, h, gate, pe):
        o_ref[rows, :] = hc + gc * pc


def _tail(x2, attn, sgu, p2, *params):
    n, d = x2.shape
    tm = TAIL_TILE

    def tile(a):
        return pl.BlockSpec((tm, a.shape[1]), lambda i: (i, 0))

    return pl.pallas_call(
        _tail_kernel,
        grid=(n // tm,),
        in_specs=[tile(x2), pl.BlockSpec((attn.shape[0], tm, LANES), lambda i: (0, i, 0)),
                  tile(sgu), tile(p2)] + [_const_spec(a.shape) for a in params],
        out_specs=pl.BlockSpec((tm, d), lambda i: (i, 0)),
        out_shape=jax.ShapeDtypeStruct((n, d), F32),
        compiler_params=pltpu.CompilerParams(dimension_semantics=("arbitrary",),
                                             vmem_limit_bytes=VMEM_LIMIT),
        name="tail",
    )(x2, attn, sgu, p2, *params)


def _group_major(w_in, attn_w, sgu_w):
    gd = sgu_w // N_SGU_GROUPS
    u0 = 3 * attn_w
    z0 = u0 + sgu_w
    cols = []
    for g in range(N_SGU_GROUPS):
        cols += [w_in[:, u0 + g * gd:u0 + (g + 1) * gd], w_in[:, z0 + g * gd:z0 + (g + 1) * gd]]
    return jnp.concatenate(cols + [w_in[:, :u0]], axis=1)


def kernel(x, p, ln_pre_mix, w_in, sgu_ln_g, sgu_ln_b, w_spatial, b_spatial, attn_out_norm,
           sgu_out_norm, w_out, ln_post_mix, ln_pre_ffn, w_gate_up, w_down, ln_post_ffn,
           w_pe_gate, b_pe_gate, w_pe_proj):
    batch, seq, d = x.shape
    depth = w_in.shape[0]
    attn_w = attn_out_norm.shape[1]
    sgu_w = sgu_out_norm.shape[1]
    n_heads = attn_w // HEAD_DIM
    slopes = 2.0 ** (-8.0 * (jnp.arange(n_heads, dtype=F32) + 1.0) / n_heads)

    def row(a):
        return a.reshape(1, -1)

    h = x.reshape(batch * seq, d)
    for i in range(depth):
        def col(a):
            return a.reshape(-1, 1)

        def unscaled(w):
            return w, jnp.ones((w.shape[0], 1), F32)

        tail_weights = ((w_out[i], col(jnp.concatenate([attn_out_norm[i], sgu_out_norm[i]]))),
                        (w_gate_up[i], col(ln_pre_ffn[i])),
                        unscaled(w_down[i]), unscaled(w_pe_gate[i]), unscaled(w_pe_proj[i]))
        w_in_b = _group_major(w_in[i] * col(ln_pre_mix[i]), attn_w, sgu_w).astype(BF16)
        outs = _proj_sgu(h, w_in_b, row(sgu_ln_g[i]), row(sgu_ln_b[i]),
                         w_spatial[i], b_spatial[i][:, :, None], attn_w, sgu_w, tail_weights)
        qkv_layouts, sgu = outs[:9], outs[9]
        wout_b, wgu_b, wd_b, wpg_b, wpp_b = outs[10:]
        attn = _attention(qkv_layouts, slopes, batch, seq)
        h = _tail(h, attn, sgu, p[i].reshape(batch * seq, -1),
                  wout_b, row(ln_post_mix[i]), wgu_b, wd_b,
                  row(ln_post_ffn[i]), wpg_b, row(b_pe_gate[i]), wpp_b)
    return h.reshape(batch, seq, d)
```

```python
import functools

import jax
import jax.numpy as jnp
from jax import lax
from jax.experimental import pallas as pl
from jax.experimental.pallas import tpu as pltpu

F32 = jnp.float32
BF16 = jnp.bfloat16

HEAD_DIM = 64
QBLK = 128
WIN = 2 * QBLK
DILATIONS = (1, 4, 16)
SPAN = QBLK * DILATIONS[-1]
SGU_CHUNK = 128
N_SGU_GROUPS = 4
EPS = 1e-6
NEG = -1e30
LOG2E = 1.4426950408889634
Q_SCALE = HEAD_DIM ** -0.5 * LOG2E

LANES = 128
BF16_SUBLANES = 16
HEADS_PER_SLAB = LANES // HEAD_DIM

PROJ_TILE = 512
TAIL_TILE = 512
TAIL_SPLITS = (256,)
VMEM_LIMIT = 56 * 1024 * 1024


def _rms(x, g):
    return x * lax.rsqrt(jnp.mean(x * x, axis=-1, keepdims=True) + EPS) * g


def _const_spec(shape):
    zeros = (0,) * len(shape)
    return pl.BlockSpec(shape, lambda *_: zeros, pipeline_mode=pl.Buffered(1))


def _proj_sgu_kernel(*refs, attn_w, sgu_w, n_cast):
    x_ref, g_ref, w_ref, lng_ref, lnb_ref, ws_ref, bs_ref, gout_ref = refs[:8]
    cast_in = refs[8:8 + n_cast]
    qn, kn, vn, qa, ka, va, qb, kb, vb, sgu_ref = refs[8 + n_cast:18 + n_cast]
    cast_out = refs[18 + n_cast:18 + 2 * n_cast]
    tok_tmp, res_tmp = refs[18 + 2 * n_cast:]
    for src, dst in zip(cast_in, cast_out):
        dst[...] = src[...].astype(dst.dtype)
    tm = x_ref.shape[0]
    gd = sgu_w // N_SGU_GROUPS
    a = _rms(x_ref[...], g_ref[...]).astype(BF16)

    def project(c0, width):
        return jnp.dot(a, w_ref[:, c0:c0 + width], preferred_element_type=F32)

    def emit_layouts(idx, n_ref, a_ref, b_ref, cols):
        for s in range(attn_w // LANES):
            slab = cols[:, s * LANES:(s + 1) * LANES]
            n_ref[s] = slab.astype(BF16)
            tok_tmp[idx, s] = slab
            for r in range(4):
                every4 = tok_tmp[idx, s, pl.ds(r, tm // 4, stride=4), :]
                a_ref[s, 0, r] = every4.astype(BF16)
                res_tmp[idx, s, r] = every4
                for r2 in range(4):
                    b_ref[s, 0, 4 * r2 + r] = (
                        res_tmp[idx, s, r, pl.ds(r2, tm // 16, stride=4), :].astype(BF16))

    row = lax.broadcasted_iota(jnp.int32, (SGU_CHUNK, SGU_CHUNK), 0)
    col = lax.broadcasted_iota(jnp.int32, (SGU_CHUNK, SGU_CHUNK), 1)
    causal = row >= col
    outs = []
    ssq = jnp.zeros((tm, 1), F32)
    uz_pairs = [project(0, 4 * gd)]
    for g in range(N_SGU_GROUPS):
        uz = uz_pairs[g // 2][:, (g % 2) * 2 * gd:(g % 2 + 1) * 2 * gd]
        if g == 0:
            uz_pairs.append(project(4 * gd, 4 * gd))
        elif g == 1:
            emit_layouts(0, qn, qa, qb, project(2 * sgu_w, attn_w) * Q_SCALE)
        elif g == 2:
            emit_layouts(1, kn, ka, kb, project(2 * sgu_w + attn_w, attn_w))
        elif g == 3:
            emit_layouts(2, vn, va, vb, project(2 * sgu_w + 2 * attn_w, attn_w))
        u = jax.nn.gelu(uz[:, :gd])
        z = jax.nn.gelu(uz[:, gd:])
        zc = z - jnp.mean(z, axis=-1, keepdims=True)
        zn = zc * lax.rsqrt(jnp.mean(zc * zc, axis=-1, keepdims=True) + EPS)
        zn = (zn * lng_ref[...] + lnb_ref[...]).astype(BF16)
        wm = jnp.where(causal, ws_ref[g], 0.0).astype(BF16)
        chunks = [zn[c * SGU_CHUNK:(c + 1) * SGU_CHUNK] for c in range(tm // SGU_CHUNK)]
        mixed = jnp.dot(wm, jnp.concatenate(chunks, axis=1), preferred_element_type=F32)
        mixed = [mixed[:, c * gd:(c + 1) * gd] + bs_ref[g] for c in range(len(chunks))]
        o = u * jnp.concatenate(mixed, axis=0)
        ssq = ssq + jnp.sum(o * o, axis=-1, keepdims=True)
        outs.append(o)
    scale = lax.rsqrt(ssq / sgu_w + EPS)
    for g in range(N_SGU_GROUPS):
        sgu_ref[:, g * gd:(g + 1) * gd] = (outs[g] * scale * gout_ref[:, g * gd:(g + 1) * gd]).astype(BF16)


def _cast_chunk_spec(w, steps):
    rows = w.shape[0]
    chunk = max(BF16_SUBLANES, -(-rows // steps // BF16_SUBLANES) * BF16_SUBLANES)
    while rows % chunk:
        chunk += BF16_SUBLANES
    last = rows // chunk - 1
    return pl.BlockSpec((chunk, w.shape[1]), lambda i: (jnp.minimum(i, last), 0))


def _proj_sgu(x2, g_pre, w_in, ln_g, ln_b, w_sp, b_sp, g_sgu, attn_w, sgu_w, to_cast):
    n, d = x2.shape
    tm = PROJ_TILE
    per_span = SPAN // tm
    cast_specs = [_cast_chunk_spec(w, n // tm) for w in to_cast]
    n_slabs = attn_w // LANES
    n_shape = jax.ShapeDtypeStruct((n_slabs, n, LANES), BF16)
    a_shape = jax.ShapeDtypeStruct((n_slabs, n // SPAN, 4, SPAN // 4, LANES), BF16)
    b_shape = jax.ShapeDtypeStruct((n_slabs, n // SPAN, 16, SPAN // 16, LANES), BF16)
    n_spec = pl.BlockSpec((n_slabs, tm, LANES), lambda i: (0, i, 0))
    a_spec = pl.BlockSpec((n_slabs, 1, 4, tm // 4, LANES),
                          lambda i: (0, i // per_span, 0, i % per_span, 0))
    b_spec = pl.BlockSpec((n_slabs, 1, 16, tm // 16, LANES),
                          lambda i: (0, i // per_span, 0, i % per_span, 0))
    return pl.pallas_call(
        functools.partial(_proj_sgu_kernel, attn_w=attn_w, sgu_w=sgu_w, n_cast=len(to_cast)),
        grid=(n // tm,),
        in_specs=[
            pl.BlockSpec((tm, d), lambda i: (i, 0)),
            _const_spec(g_pre.shape), _const_spec(w_in.shape), _const_spec(ln_g.shape),
            _const_spec(ln_b.shape), _const_spec(w_sp.shape), _const_spec(b_sp.shape),
            _const_spec(g_sgu.shape),
        ] + cast_specs,
        out_specs=[n_spec] * 3 + [a_spec] * 3 + [b_spec] * 3
        + [pl.BlockSpec((tm, sgu_w), lambda i: (i, 0))] + cast_specs,
        out_shape=[n_shape] * 3 + [a_shape] * 3 + [b_shape] * 3
        + [jax.ShapeDtypeStruct((n, sgu_w), BF16)]
        + [jax.ShapeDtypeStruct(w.shape, BF16) for w in to_cast],
        scratch_shapes=[pltpu.VMEM((3, n_slabs, tm, LANES), F32),
                        pltpu.VMEM((3, n_slabs, 4, tm // 4, LANES), F32)],
        compiler_params=pltpu.CompilerParams(dimension_semantics=("arbitrary",),
                                             vmem_limit_bytes=VMEM_LIMIT),
        name="proj_sgu",
    )(x2, g_pre, w_in, ln_g, ln_b, w_sp, b_sp, g_sgu, *to_cast)


A_BLOCKS = SPAN // 4 // QBLK


def _attn_kernel(slopes_ref, qn, kn, vn, knp, vnp, qa, ka, va, kap, vap, qb, kb, vb, kbp, vbp,
                 o_ref, bias_ref, ob, lb):
    slab = pl.program_id(1)
    span = pl.program_id(2)
    lane = lax.broadcasted_iota(jnp.int32, (QBLK, LANES), 1)
    head0 = lane < HEAD_DIM
    first = span == 0

    @pl.when(first)
    def _start_of_sequence():
        i = lax.broadcasted_iota(jnp.int32, (WIN, WIN), 0)
        j = lax.broadcasted_iota(jnp.int32, (WIN, WIN), 1)
        steps = QBLK + (i & (QBLK - 1)) - j
        valid = (steps >= 0) & (steps <= QBLK)
        slope = jnp.where(i < QBLK, slopes_ref[HEADS_PER_SLAB * slab],
                          slopes_ref[HEADS_PER_SLAB * slab + 1])
        for b, d in enumerate(DILATIONS):
            bias = jnp.where(valid, -(slope * LOG2E) * (steps * d).astype(F32), NEG)
            bias_ref[b, 1] = bias
            bias_ref[b, 0] = jnp.where(j >= QBLK, bias, NEG)

    not_first = jnp.where(first, 0, 1)
    ones = jnp.ones((WIN, LANES), BF16)

    def window(prev_blk, cur_ref, idx, nb):
        if nb > 0:
            return cur_ref[idx + (slice((nb - 1) * QBLK, (nb + 1) * QBLK), slice(None))]
        prev = jnp.where(first, jnp.zeros_like(prev_blk), prev_blk)
        return jnp.concatenate([prev, cur_ref[idx + (slice(0, QBLK), slice(None))]], axis=0)

    def block(b, q, kw, vw, nb):
        zero = jnp.zeros_like(q)
        qq = jnp.concatenate([jnp.where(head0, q, zero), jnp.where(head0, zero, q)], axis=0)
        s = lax.dot_general(qq, kw, (((1,), (1,)), ((), ())), preferred_element_type=F32)
        s = s + bias_ref[b, not_first if nb == 0 else 1]
        m = jnp.max(s, axis=1, keepdims=True)
        p = jnp.exp2(s - m).astype(BF16)
        acc = jnp.dot(p, jnp.concatenate([vw, ones], axis=1), preferred_element_type=F32)
        o_t = jnp.where(head0, acc[0:QBLK, 0:LANES], acc[QBLK:WIN, 0:LANES])
        l_t = jnp.where(head0, acc[0:QBLK, LANES:], acc[QBLK:WIN, LANES:])
        m_t = jnp.where(head0, m[0:QBLK], m[QBLK:WIN])
        return o_t / l_t, m_t + jnp.log2(l_t)

    for nb in range(SPAN // QBLK):
        q = qn[0, nb * QBLK:(nb + 1) * QBLK, :]
        o_t, lse_t = block(0, q, window(knp[0], kn, (0,), nb), window(vnp[0], vn, (0,), nb), nb)
        ob[0, nb * QBLK:(nb + 1) * QBLK, :] = o_t
        lb[0, nb * QBLK:(nb + 1) * QBLK, :] = lse_t
    for r in range(4):
        for nb in range(A_BLOCKS):
            q = qa[0, 0, r, nb * QBLK:(nb + 1) * QBLK, :]
            o_t, lse_t = block(1, q, window(kap[0, 0, r], ka, (0, 0, r), nb),
                               window(vap[0, 0, r], va, (0, 0, r), nb), nb)
            rows = slice((r * A_BLOCKS + nb) * QBLK, (r * A_BLOCKS + nb + 1) * QBLK)
            ob[1, rows, :] = o_t
            lb[1, rows, :] = lse_t
    for r in range(16):
        o_t, lse_t = block(2, qb[0, 0, r], window(kbp[0, 0, r], kb, (0, 0, r), 0),
                           window(vbp[0, 0, r], vb, (0, 0, r), 0), 0)
        rows = pl.ds((r % 4) * (SPAN // 4) + r // 4, QBLK, stride=4)
        ob[2, rows, :] = o_t
        lb[2, rows, :] = lse_t

    for c in range(SPAN // WIN):
        rows = slice(c * WIN, (c + 1) * WIN)
        tokens = pl.ds(c // 2 + (c % 2) * (4 * WIN), WIN, stride=4)
        l0, l1, l2 = lb[0, tokens, :], lb[1, rows, :], lb[2, rows, :]
        mx = jnp.maximum(jnp.maximum(l0, l1), l2)
        w0, w1, w2 = jnp.exp2(l0 - mx), jnp.exp2(l1 - mx), jnp.exp2(l2 - mx)
        num = w0 * ob[0, tokens, :] + w1 * ob[1, rows, :] + w2 * ob[2, rows, :]
        o_ref[0, tokens, :] = num / (w0 + w1 + w2)


def _attention(qkv_layouts, slopes, batch, seq):
    qn, kn, vn, qa, ka, va, qb, kb, vb = qkv_layouts
    n_slabs, n, _ = qn.shape
    assert seq % SPAN == 0
    spans = seq // SPAN

    def cur(block_shape):
        zeros = (0,) * (len(block_shape) - 2)
        return pl.BlockSpec(block_shape, lambda b, s, t: (s, b * spans + t) + zeros)

    n_spec = cur((1, SPAN, LANES))
    a_spec = cur((1, 1, 4, SPAN // 4, LANES))
    b_spec = cur((1, 1, 16, QBLK, LANES))
    np_spec = pl.BlockSpec((1, QBLK, LANES),
                           lambda b, s, t: (s, jnp.maximum((b * spans + t) * (SPAN // QBLK) - 1, 0), 0))
    ap_spec = pl.BlockSpec((1, 1, 4, QBLK, LANES),
                           lambda b, s, t: (s, jnp.maximum(b * spans + t - 1, 0), 0, A_BLOCKS - 1, 0))
    bp_spec = pl.BlockSpec((1, 1, 16, QBLK, LANES),
                           lambda b, s, t: (s, jnp.maximum(b * spans + t - 1, 0), 0, 0, 0))
    return pl.pallas_call(
        _attn_kernel,
        grid=(batch, n_slabs, spans),
        in_specs=[pl.BlockSpec(memory_space=pltpu.SMEM),
                  n_spec, n_spec, n_spec, np_spec, np_spec,
                  a_spec, a_spec, a_spec, ap_spec, ap_spec,
                  b_spec, b_spec, b_spec, bp_spec, bp_spec],
        out_specs=pl.BlockSpec((1, SPAN, LANES), lambda b, s, t: (s, b * spans + t, 0)),
        out_shape=jax.ShapeDtypeStruct((n_slabs, n, LANES), F32),
        scratch_shapes=[
            pltpu.VMEM((len(DILATIONS), 2, WIN, WIN), F32),
            pltpu.VMEM((len(DILATIONS), SPAN, LANES), F32),
            pltpu.VMEM((len(DILATIONS), SPAN, LANES), F32),
        ],
        compiler_params=pltpu.CompilerParams(
            dimension_semantics=("arbitrary", "arbitrary", "arbitrary"),
            vmem_limit_bytes=VMEM_LIMIT),
        name="attn",
    )(slopes, qn, kn, vn, kn, vn, qa, ka, va, ka, va, qb, kb, vb, kb, vb)


def _tail_kernel(x_ref, attn_ref, sgu_ref, p_ref, ga_ref, wout_ref, gpm_ref, gpf_ref,
                 wgu_ref, wd_ref, gpo_ref, wpg_ref, bpg_ref, wpp_ref, o_ref):
    d_ff = wd_ref.shape[0]
    bounds = (0,) + TAIL_SPLITS + (x_ref.shape[0],)
    chains = [slice(lo, hi) for lo, hi in zip(bounds[:-1], bounds[1:])]

    def dot(a, w_ref):
        return jnp.dot(a, w_ref[...], preferred_element_type=F32)

    pe = [dot(p_ref[rows, :].astype(BF16), wpp_ref) for rows in chains]
    groups = []
    for rows in chains:
        attn = jnp.concatenate([attn_ref[s, rows, :] for s in range(attn_ref.shape[0])], axis=-1)
        an = _rms(attn, ga_ref[...]).astype(BF16)
        groups.append(jnp.concatenate([an, sgu_ref[rows, :]], axis=-1))
    mixed = [dot(g, wout_ref) for g in groups]
    h = [x_ref[rows, :] + _rms(m, gpm_ref[...]) for rows, m in zip(chains, mixed)]
    f = [_rms(hc, gpf_ref[...]).astype(BF16) for hc in h]
    gu = [dot(fc, wgu_ref) for fc in f]
    hid = [(jax.nn.silu(g[:, :d_ff]) * g[:, d_ff:]).astype(BF16) for g in gu]
    y = [dot(hc, wd_ref) for hc in hid]
    h = [hc + _rms(yc, gpo_ref[...]) for hc, yc in zip(h, y)]
    gate = [jax.nn.sigmoid(dot(hc.astype(BF16), wpg_ref) + bpg_ref[...]) for hc in h]
    for rows, hc, gc, pc in zip(chains, h, gate, pe):
        o_ref[rows, :] = hc + gc * pc


def _tail(x2, attn, sgu, p2, *params):
    n, d = x2.shape
    tm = TAIL_TILE

    def tile(a):
        return pl.BlockSpec((tm, a.shape[1]), lambda i: (i, 0))

    return pl.pallas_call(
        _tail_kernel,
        grid=(n // tm,),
        in_specs=[tile(x2), pl.BlockSpec((attn.shape[0], tm, LANES), lambda i: (0, i, 0)),
                  tile(sgu), tile(p2)] + [_const_spec(a.shape) for a in params],
        out_specs=pl.BlockSpec((tm, d), lambda i: (i, 0)),
        out_shape=jax.ShapeDtypeStruct((n, d), F32),
        compiler_params=pltpu.CompilerParams(dimension_semantics=("arbitrary",),
                                             vmem_limit_bytes=VMEM_LIMIT),
        name="tail",
    )(x2, attn, sgu, p2, *params)


def _group_major(w_in, attn_w, sgu_w):
    gd = sgu_w // N_SGU_GROUPS
    u0 = 3 * attn_w
    z0 = u0 + sgu_w
    cols = []
    for g in range(N_SGU_GROUPS):
        cols += [w_in[:, u0 + g * gd:u0 + (g + 1) * gd], w_in[:, z0 + g * gd:z0 + (g + 1) * gd]]
    return jnp.concatenate(cols + [w_in[:, :u0]], axis=1)


def kernel(x, p, ln_pre_mix, w_in, sgu_ln_g, sgu_ln_b, w_spatial, b_spatial, attn_out_norm,
           sgu_out_norm, w_out, ln_post_mix, ln_pre_ffn, w_gate_up, w_down, ln_post_ffn,
           w_pe_gate, b_pe_gate, w_pe_proj):
    batch, seq, d = x.shape
    depth = w_in.shape[0]
    attn_w = attn_out_norm.shape[1]
    sgu_w = sgu_out_norm.shape[1]
    n_heads = attn_w // HEAD_DIM
    slopes = 2.0 ** (-8.0 * (jnp.arange(n_heads, dtype=F32) + 1.0) / n_heads)

    def row(a):
        return a.reshape(1, -1)

    h = x.reshape(batch * seq, d)
    for i in range(depth):
        tail_weights = (w_out[i], w_gate_up[i], w_down[i], w_pe_gate[i], w_pe_proj[i])
        outs = _proj_sgu(
            h, row(ln_pre_mix[i]), _group_major(w_in[i], attn_w, sgu_w).astype(BF16),
            row(sgu_ln_g[i]), row(sgu_ln_b[i]),
            w_spatial[i], b_spatial[i][:, :, None], row(sgu_out_norm[i]), attn_w, sgu_w,
            tail_weights)
        qkv_layouts, sgu = outs[:9], outs[9]
        wout_b, wgu_b, wd_b, wpg_b, wpp_b = outs[10:]
        attn = _attention(qkv_layouts, slopes, batch, seq)
        h = _tail(h, attn, sgu, p[i].reshape(batch * seq, -1),
                  row(attn_out_norm[i]), wout_b, row(ln_post_mix[i]),
                  row(ln_pre_ffn[i]), wgu_b, wd_b,
                  row(ln_post_ffn[i]), wpg_b, row(b_pe_gate[i]), wpp_b)
    return h.reshape(batch, seq, d)
```

```python
import functools

import jax
import jax.numpy as jnp
from jax import lax
from jax.experimental import pallas as pl
from jax.experimental.pallas import tpu as pltpu

F32 = jnp.float32
BF16 = jnp.bfloat16

HEAD_DIM = 64
QBLK = 128
WIN = 2 * QBLK
DILATIONS = (1, 4, 16)
SPAN = QBLK * DILATIONS[-1]
SGU_CHUNK = 128
N_SGU_GROUPS = 4
EPS = 1e-6
NEG = -1e30
LOG2E = 1.4426950408889634
Q_SCALE = HEAD_DIM ** -0.5 * LOG2E

LANES = 128
BF16_SUBLANES = 16
HEADS_PER_SLAB = LANES // HEAD_DIM

PROJ_TILE = 512
TAIL_TILE = 512
TAIL_SPLITS = (256,)
VMEM_LIMIT = 56 * 1024 * 1024


def _rms(x, g):
    return x * lax.rsqrt(jnp.mean(x * x, axis=-1, keepdims=True) + EPS) * g


def _const_spec(shape):
    zeros = (0,) * len(shape)
    return pl.BlockSpec(shape, lambda *_: zeros, pipeline_mode=pl.Buffered(1))


def _proj_sgu_kernel(*refs, attn_w, sgu_w, n_cast):
    x_ref, g_ref, w_ref, lng_ref, lnb_ref, ws_ref, bs_ref, gout_ref = refs[:8]
    cast_in = refs[8:8 + n_cast]
    qa, ka, va, qb, kb, vb, sgu_ref = refs[8 + n_cast:15 + n_cast]
    cast_out = refs[15 + n_cast:15 + 2 * n_cast]
    tok_tmp, res_tmp = refs[15 + 2 * n_cast:]
    for src, dst in zip(cast_in, cast_out):
        dst[...] = src[...].astype(dst.dtype)
    tm = x_ref.shape[0]
    gd = sgu_w // N_SGU_GROUPS
    a = _rms(x_ref[...], g_ref[...]).astype(BF16)

    def project(c0, width):
        return jnp.dot(a, w_ref[:, c0:c0 + width], preferred_element_type=F32)

    def emit_layouts(idx, a_ref, b_ref, cols):
        for s in range(attn_w // LANES):
            tok_tmp[idx, s] = cols[:, s * LANES:(s + 1) * LANES]
            for r in range(4):
                every4 = tok_tmp[idx, s, pl.ds(r, tm // 4, stride=4), :]
                a_ref[s, 0, r] = every4.astype(BF16)
                res_tmp[idx, s, r] = every4
                for r2 in range(4):
                    b_ref[s, 0, 4 * r2 + r] = (
                        res_tmp[idx, s, r, pl.ds(r2, tm // 16, stride=4), :].astype(BF16))

    row = lax.broadcasted_iota(jnp.int32, (SGU_CHUNK, SGU_CHUNK), 0)
    col = lax.broadcasted_iota(jnp.int32, (SGU_CHUNK, SGU_CHUNK), 1)
    causal = row >= col
    outs = []
    ssq = jnp.zeros((tm, 1), F32)
    uz_pairs = [project(0, 4 * gd)]
    for g in range(N_SGU_GROUPS):
        uz = uz_pairs[g // 2][:, (g % 2) * 2 * gd:(g % 2 + 1) * 2 * gd]
        if g == 0:
            uz_pairs.append(project(4 * gd, 4 * gd))
        elif g == 1:
            emit_layouts(0, qa, qb, project(2 * sgu_w, attn_w) * Q_SCALE)
        elif g == 2:
            emit_layouts(1, ka, kb, project(2 * sgu_w + attn_w, attn_w))
        elif g == 3:
            emit_layouts(2, va, vb, project(2 * sgu_w + 2 * attn_w, attn_w))
        u = jax.nn.gelu(uz[:, :gd])
        z = jax.nn.gelu(uz[:, gd:])
        zc = z - jnp.mean(z, axis=-1, keepdims=True)
        zn = zc * lax.rsqrt(jnp.mean(zc * zc, axis=-1, keepdims=True) + EPS)
        zn = (zn * lng_ref[...] + lnb_ref[...]).astype(BF16)
        wm = jnp.where(causal, ws_ref[g], 0.0).astype(BF16)
        chunks = [zn[c * SGU_CHUNK:(c + 1) * SGU_CHUNK] for c in range(tm // SGU_CHUNK)]
        mixed = jnp.dot(wm, jnp.concatenate(chunks, axis=1), preferred_element_type=F32)
        mixed = [mixed[:, c * gd:(c + 1) * gd] + bs_ref[g] for c in range(len(chunks))]
        o = u * jnp.concatenate(mixed, axis=0)
        ssq = ssq + jnp.sum(o * o, axis=-1, keepdims=True)
        outs.append(o)
    scale = lax.rsqrt(ssq / sgu_w + EPS)
    for g in range(N_SGU_GROUPS):
        sgu_ref[:, g * gd:(g + 1) * gd] = (outs[g] * scale * gout_ref[:, g * gd:(g + 1) * gd]).astype(BF16)


def _cast_chunk_spec(w, steps):
    rows = w.shape[0]
    chunk = max(BF16_SUBLANES, -(-rows // steps // BF16_SUBLANES) * BF16_SUBLANES)
    while rows % chunk:
        chunk += BF16_SUBLANES
    last = rows // chunk - 1
    return pl.BlockSpec((chunk, w.shape[1]), lambda i: (jnp.minimum(i, last), 0))


def _proj_sgu(x2, g_pre, w_in, ln_g, ln_b, w_sp, b_sp, g_sgu, attn_w, sgu_w, to_cast):
    n, d = x2.shape
    tm = PROJ_TILE
    per_span = SPAN // tm
    cast_specs = [_cast_chunk_spec(w, n // tm) for w in to_cast]
    n_slabs = attn_w // LANES
    a_shape = jax.ShapeDtypeStruct((n_slabs, n // SPAN, 4, SPAN // 4, LANES), BF16)
    b_shape = jax.ShapeDtypeStruct((n_slabs, n // SPAN, 16, SPAN // 16, LANES), BF16)
    a_spec = pl.BlockSpec((n_slabs, 1, 4, tm // 4, LANES),
                          lambda i: (0, i // per_span, 0, i % per_span, 0))
    b_spec = pl.BlockSpec((n_slabs, 1, 16, tm // 16, LANES),
                          lambda i: (0, i // per_span, 0, i % per_span, 0))
    return pl.pallas_call(
        functools.partial(_proj_sgu_kernel, attn_w=attn_w, sgu_w=sgu_w, n_cast=len(to_cast)),
        grid=(n // tm,),
        in_specs=[
            pl.BlockSpec((tm, d), lambda i: (i, 0)),
            _const_spec(g_pre.shape), _const_spec(w_in.shape), _const_spec(ln_g.shape),
            _const_spec(ln_b.shape), _const_spec(w_sp.shape), _const_spec(b_sp.shape),
            _const_spec(g_sgu.shape),
        ] + cast_specs,
        out_specs=[a_spec] * 3 + [b_spec] * 3
        + [pl.BlockSpec((tm, sgu_w), lambda i: (i, 0))] + cast_specs,
        out_shape=[a_shape] * 3 + [b_shape] * 3
        + [jax.ShapeDtypeStruct((n, sgu_w), BF16)]
        + [jax.ShapeDtypeStruct(w.shape, BF16) for w in to_cast],
        scratch_shapes=[pltpu.VMEM((3, n_slabs, tm, LANES), F32),
                        pltpu.VMEM((3, n_slabs, 4, tm // 4, LANES), F32)],
        compiler_params=pltpu.CompilerParams(dimension_semantics=("arbitrary",),
                                             vmem_limit_bytes=VMEM_LIMIT),
        name="proj_sgu",
    )(x2, g_pre, w_in, ln_g, ln_b, w_sp, b_sp, g_sgu, *to_cast)


A_BLOCKS = SPAN // 4 // QBLK
PIECE = QBLK // 4


def _attn_kernel(slopes_ref, qa, ka, va, kap, vap, qb, kb, vb, kbp, vbp,
                 o_ref, bias_ref, ob, lb):
    slab = pl.program_id(1)
    span = pl.program_id(2)
    lane = lax.broadcasted_iota(jnp.int32, (QBLK, LANES), 1)
    head0 = lane < HEAD_DIM
    first = span == 0

    @pl.when(first)
    def _start_of_sequence():
        i = lax.broadcasted_iota(jnp.int32, (WIN, WIN), 0)
        j = lax.broadcasted_iota(jnp.int32, (WIN, WIN), 1)
        slope = jnp.where(i < QBLK, slopes_ref[HEADS_PER_SLAB * slab],
                          slopes_ref[HEADS_PER_SLAB * slab + 1])
        q_pos = i & (QBLK - 1)
        for b, d in enumerate(DILATIONS):
            if d == 1:
                q_at = 4 * (q_pos & (PIECE - 1)) + q_pos // PIECE
                k_at = 4 * (j & (2 * PIECE - 1)) + j // (2 * PIECE)
            else:
                q_at, k_at = q_pos, j
            steps = QBLK + q_at - k_at
            valid = (steps >= 0) & (steps <= QBLK)
            bias = jnp.where(valid, -(slope * LOG2E) * (steps * d).astype(F32), NEG)
            bias_ref[b, 1] = bias
            bias_ref[b, 0] = jnp.where(k_at >= QBLK, bias, NEG)

    not_first = jnp.where(first, 0, 1)
    ones = jnp.ones((WIN, LANES), BF16)

    def before_sequence(prev_blk):
        return jnp.where(first, jnp.zeros_like(prev_blk), prev_blk)

    def window(prev_blk, cur_ref, idx, nb):
        if nb > 0:
            return cur_ref[idx + (slice((nb - 1) * QBLK, (nb + 1) * QBLK), slice(None))]
        return jnp.concatenate([before_sequence(prev_blk),
                                cur_ref[idx + (slice(0, QBLK), slice(None))]], axis=0)

    def window1(prev_ref, cur_ref, nb):
        pieces = []
        for r in range(4):
            if nb > 0:
                pieces.append(cur_ref[0, 0, r, (nb - 1) * PIECE:(nb + 1) * PIECE, :])
            else:
                pieces += [before_sequence(prev_ref[0, 0, r, QBLK - PIECE:QBLK, :]),
                           cur_ref[0, 0, r, 0:PIECE, :]]
        return jnp.concatenate(pieces, axis=0)

    def block(b, q, kw, vw, nb):
        zero = jnp.zeros_like(q)
        qq = jnp.concatenate([jnp.where(head0, q, zero), jnp.where(head0, zero, q)], axis=0)
        s = lax.dot_general(qq, kw, (((1,), (1,)), ((), ())), preferred_element_type=F32)
        s = s + bias_ref[b, not_first if nb == 0 else 1]
        m = jnp.max(s, axis=1, keepdims=True)
        p = jnp.exp2(s - m).astype(BF16)
        acc = jnp.dot(p, jnp.concatenate([vw, ones], axis=1), preferred_element_type=F32)
        o_t = jnp.where(head0, acc[0:QBLK, 0:LANES], acc[QBLK:WIN, 0:LANES])
        l_t = jnp.where(head0, acc[0:QBLK, LANES:], acc[QBLK:WIN, LANES:])
        m_t = jnp.where(head0, m[0:QBLK], m[QBLK:WIN])
        return o_t / l_t, m_t + jnp.log2(l_t)

    for nb in range(SPAN // QBLK):
        q = jnp.concatenate([qa[0, 0, r, nb * PIECE:(nb + 1) * PIECE, :] for r in range(4)], axis=0)
        o_t, lse_t = block(0, q, window1(kap, ka, nb), window1(vap, va, nb), nb)
        for r in range(4):
            rows = slice(r * (SPAN // 4) + nb * PIECE, r * (SPAN // 4) + (nb + 1) * PIECE)
            ob[0, rows, :] = o_t[r * PIECE:(r + 1) * PIECE]
            lb[0, rows, :] = lse_t[r * PIECE:(r + 1) * PIECE]
    for r in range(4):
        for nb in range(A_BLOCKS):
            q = qa[0, 0, r, nb * QBLK:(nb + 1) * QBLK, :]
            o_t, lse_t = block(1, q, window(kap[0, 0, r], ka, (0, 0, r), nb),
                               window(vap[0, 0, r], va, (0, 0, r), nb), nb)
            rows = slice((r * A_BLOCKS + nb) * QBLK, (r * A_BLOCKS + nb + 1) * QBLK)
            ob[1, rows, :] = o_t
            lb[1, rows, :] = lse_t
    for r in range(16):
        o_t, lse_t = block(2, qb[0, 0, r], window(kbp[0, 0, r], kb, (0, 0, r), 0),
                           window(vbp[0, 0, r], vb, (0, 0, r), 0), 0)
        rows = pl.ds((r % 4) * (SPAN // 4) + r // 4, QBLK, stride=4)
        ob[2, rows, :] = o_t
        lb[2, rows, :] = lse_t

    for c in range(SPAN // WIN):
        rows = slice(c * WIN, (c + 1) * WIN)
        tokens = pl.ds(c // 2 + (c % 2) * (4 * WIN), WIN, stride=4)
        l0, l1, l2 = lb[0, rows, :], lb[1, rows, :], lb[2, rows, :]
        mx = jnp.maximum(jnp.maximum(l0, l1), l2)
        w0, w1, w2 = jnp.exp2(l0 - mx), jnp.exp2(l1 - mx), jnp.exp2(l2 - mx)
        num = w0 * ob[0, rows, :] + w1 * ob[1, rows, :] + w2 * ob[2, rows, :]
        o_ref[0, tokens, :] = num / (w0 + w1 + w2)


def _attention(qkv_layouts, slopes, batch, seq):
    qa, ka, va, qb, kb, vb = qkv_layouts
    n_slabs = qa.shape[0]
    n = batch * seq
    assert seq % SPAN == 0
    spans = seq // SPAN

    def cur(block_shape):
        zeros = (0,) * (len(block_shape) - 2)
        return pl.BlockSpec(block_shape, lambda b, s, t: (s, b * spans + t) + zeros)

    a_spec = cur((1, 1, 4, SPAN // 4, LANES))
    b_spec = cur((1, 1, 16, QBLK, LANES))
    ap_spec = pl.BlockSpec((1, 1, 4, QBLK, LANES),
                           lambda b, s, t: (s, jnp.maximum(b * spans + t - 1, 0), 0, A_BLOCKS - 1, 0))
    bp_spec = pl.BlockSpec((1, 1, 16, QBLK, LANES),
                           lambda b, s, t: (s, jnp.maximum(b * spans + t - 1, 0), 0, 0, 0))
    return pl.pallas_call(
        _attn_kernel,
        grid=(batch, n_slabs, spans),
        in_specs=[pl.BlockSpec(memory_space=pltpu.SMEM),
                  a_spec, a_spec, a_spec, ap_spec, ap_spec,
                  b_spec, b_spec, b_spec, bp_spec, bp_spec],
        out_specs=pl.BlockSpec((1, SPAN, LANES), lambda b, s, t: (s, b * spans + t, 0)),
        out_shape=jax.ShapeDtypeStruct((n_slabs, n, LANES), F32),
        scratch_shapes=[
            pltpu.VMEM((len(DILATIONS), 2, WIN, WIN), F32),
            pltpu.VMEM((len(DILATIONS), SPAN, LANES), F32),
            pltpu.VMEM((len(DILATIONS), SPAN, LANES), F32),
        ],
        compiler_params=pltpu.CompilerParams(
            dimension_semantics=("arbitrary", "arbitrary", "arbitrary"),
            vmem_limit_bytes=VMEM_LIMIT),
        name="attn",
    )(slopes, qa, ka, va, ka, va, qb, kb, vb, kb, vb)


def _tail_kernel(x_ref, attn_ref, sgu_ref, p_ref, ga_ref, wout_ref, gpm_ref, gpf_ref,
                 wgu_ref, wd_ref, gpo_ref, wpg_ref, bpg_ref, wpp_ref, o_ref):
    d_ff = wd_ref.shape[0]
    bounds = (0,) + TAIL_SPLITS + (x_ref.shape[0],)
    chains = [slice(lo, hi) for lo, hi in zip(bounds[:-1], bounds[1:])]

    def dot(a, w_ref):
        return jnp.dot(a, w_ref[...], preferred_element_type=F32)

    pe = [dot(p_ref[rows, :].astype(BF16), wpp_ref) for rows in chains]
    groups = []
    for rows in chains:
        attn = jnp.concatenate([attn_ref[s, rows, :] for s in range(attn_ref.shape[0])], axis=-1)
        an = _rms(attn, ga_ref[...]).astype(BF16)
        groups.append(jnp.concatenate([an, sgu_ref[rows, :]], axis=-1))
    mixed = [dot(g, wout_ref) for g in groups]
    h = [x_ref[rows, :] + _rms(m, gpm_ref[...]) for rows, m in zip(chains, mixed)]
    f = [_rms(hc, gpf_ref[...]).astype(BF16) for hc in h]
    gu = [dot(fc, wgu_ref) for fc in f]
    hid = [(jax.nn.silu(g[:, :d_ff]) * g[:, d_ff:]).astype(BF16) for g in gu]
    y = [dot(hc, wd_ref) for hc in hid]
    h = [hc + _rms(yc, gpo_ref[...]) for hc, yc in zip(h, y)]
    gate = [jax.nn.sigmoid(dot(hc.astype(BF16), wpg_ref) + bpg_ref[...]) for hc in h]
    for rows, hc, gc, pc in zip(chains, h, gate, pe):
        o_ref[rows, :] = hc + gc * pc


def _tail(x2, attn, sgu, p2, *params):
    n, d = x2.shape
    tm = TAIL_TILE

    def tile(a):
        return pl.BlockSpec((tm, a.shape[1]), lambda i: (i, 0))

    return pl.pallas_call(
        _tail_kernel,
        grid=(n // tm,),
        in_specs=[tile(x2), pl.BlockSpec((attn.shape[0], tm, LANES), lambda i: (0, i, 0)),
                  tile(sgu), tile(p2)] + [_const_spec(a.shape) for a in params],
        out_specs=pl.BlockSpec((tm, d), lambda i: (i, 0)),
        out_shape=jax.ShapeDtypeStruct((n, d), F32),
        compiler_params=pltpu.CompilerParams(dimension_semantics=("arbitrary",),
                                             vmem_limit_bytes=VMEM_LIMIT),
        name="tail",
    )(x2, attn, sgu, p2, *params)


def _group_major(w_in, attn_w, sgu_w):
    gd = sgu_w // N_SGU_GROUPS
    u0 = 3 * attn_w
    z0 = u0 + sgu_w
    cols = []
    for g in range(N_SGU_GROUPS):
        cols += [w_in[:, u0 + g * gd:u0 + (g + 1) * gd], w_in[:, z0 + g * gd:z0 + (g + 1) * gd]]
    return jnp.concatenate(cols + [w_in[:, :u0]], axis=1)


def kernel(x, p, ln_pre_mix, w_in, sgu_ln_g, sgu_ln_b, w_spatial, b_spatial, attn_out_norm,
           sgu_out_norm, w_out, ln_post_mix, ln_pre_ffn, w_gate_up, w_down, ln_post_ffn,
           w_pe_gate, b_pe_gate, w_pe_proj):
    batch, seq, d = x.shape
    depth = w_in.shape[0]
    attn_w = attn_out_norm.shape[1]
    sgu_w = sgu_out_norm.shape[1]
    n_heads = attn_w // HEAD_DIM
    slopes = 2.0 ** (-8.0 * (jnp.arange(n_heads, dtype=F32) + 1.0) / n_heads)

    def row(a):
        return a.reshape(1, -1)

    h = x.reshape(batch * seq, d)
    for i in range(depth):
        tail_weights = (w_out[i], w_gate_up[i], w_down[i], w_pe_gate[i], w_pe_proj[i])
        outs = _proj_sgu(
            h, row(ln_pre_mix[i]), _group_major(w_in[i], attn_w, sgu_w).astype(BF16),
            row(sgu_ln_g[i]), row(sgu_ln_b[i]),
            w_spatial[i], b_spatial[i][:, :, None], row(sgu_out_norm[i]), attn_w, sgu_w,
            tail_weights)
        qkv_layouts, sgu = outs[:6], outs[6]
        wout_b, wgu_b, wd_b, wpg_b, wpp_b = outs[7:]
        attn = _attention(qkv_layouts, slopes, batch, seq)
        h = _tail(h, attn, sgu, p[i].reshape(batch * seq, -1),
                  row(attn_out_norm[i]), wout_b, row(ln_post_mix[i]),
                  row(ln_pre_ffn[i]), wgu_b, wd_b,
                  row(ln_post_ffn[i]), wpg_b, row(b_pe_gate[i]), wpp_b)
    return h.reshape(batch, seq, d)
```

```python
import functools

import jax
import jax.numpy as jnp
from jax import lax
from jax.experimental import pallas as pl
from jax.experimental.pallas import tpu as pltpu

F32 = jnp.float32
BF16 = jnp.bfloat16

HEAD_DIM = 64
QBLK = 128
WIN = 2 * QBLK
DILATIONS = (1, 4, 16)
SPAN = QBLK * DILATIONS[-1]
SGU_CHUNK = 128
N_SGU_GROUPS = 4
EPS = 1e-6
NEG = -1e30
LOG2E = 1.4426950408889634
Q_SCALE = HEAD_DIM ** -0.5 * LOG2E

LANES = 128
BF16_SUBLANES = 16
HEADS_PER_SLAB = LANES // HEAD_DIM

PROJ_TILE = 512
TAIL_TILE = 512
TAIL_SPLITS = (176, 352)
VMEM_LIMIT = 56 * 1024 * 1024


def _rms(x, g):
    return x * lax.rsqrt(jnp.mean(x * x, axis=-1, keepdims=True) + EPS) * g


def _const_spec(shape):
    zeros = (0,) * len(shape)
    return pl.BlockSpec(shape, lambda *_: zeros, pipeline_mode=pl.Buffered(1))


def _proj_sgu_kernel(*refs, attn_w, sgu_w, n_cast):
    x_ref, g_ref, w_ref, lng_ref, lnb_ref, ws_ref, bs_ref, gout_ref = refs[:8]
    cast_in = refs[8:8 + n_cast]
    qa, ka, va, qb, kb, vb, sgu_ref = refs[8 + n_cast:15 + n_cast]
    cast_out = refs[15 + n_cast:15 + 2 * n_cast]
    tok_tmp, res_tmp = refs[15 + 2 * n_cast:]
    for src, dst in zip(cast_in, cast_out):
        dst[...] = src[...].astype(dst.dtype)
    tm = x_ref.shape[0]
    gd = sgu_w // N_SGU_GROUPS
    a = _rms(x_ref[...], g_ref[...]).astype(BF16)

    def project(c0, width):
        return jnp.dot(a, w_ref[:, c0:c0 + width], preferred_element_type=F32)

    def emit_layouts(idx, a_ref, b_ref, cols):
        for s in range(attn_w // LANES):
            tok_tmp[idx, s] = cols[:, s * LANES:(s + 1) * LANES]
            for r in range(4):
                every4 = tok_tmp[idx, s, pl.ds(r, tm // 4, stride=4), :]
                a_ref[s, 0, r] = every4.astype(BF16)
                res_tmp[idx, s, r] = every4
                for r2 in range(4):
                    b_ref[s, 0, 4 * r2 + r] = (
                        res_tmp[idx, s, r, pl.ds(r2, tm // 16, stride=4), :].astype(BF16))

    row = lax.broadcasted_iota(jnp.int32, (SGU_CHUNK, SGU_CHUNK), 0)
    col = lax.broadcasted_iota(jnp.int32, (SGU_CHUNK, SGU_CHUNK), 1)
    causal = row >= col
    outs = []
    ssq = jnp.zeros((tm, 1), F32)
    uz_pairs = [project(0, 4 * gd)]
    for g in range(N_SGU_GROUPS):
        uz = uz_pairs[g // 2][:, (g % 2) * 2 * gd:(g % 2 + 1) * 2 * gd]
        if g == 0:
            uz_pairs.append(project(4 * gd, 4 * gd))
        elif g == 1:
            emit_layouts(0, qa, qb, project(2 * sgu_w, attn_w) * Q_SCALE)
        elif g == 2:
            emit_layouts(1, ka, kb, project(2 * sgu_w + attn_w, attn_w))
        elif g == 3:
            emit_layouts(2, va, vb, project(2 * sgu_w + 2 * attn_w, attn_w))
        u = jax.nn.gelu(uz[:, :gd])
        z = jax.nn.gelu(uz[:, gd:])
        zc = z - jnp.mean(z, axis=-1, keepdims=True)
        zn = zc * lax.rsqrt(jnp.mean(zc * zc, axis=-1, keepdims=True) + EPS)
        zn = (zn * lng_ref[...] + lnb_ref[...]).astype(BF16)
        wm = jnp.where(causal, ws_ref[g], 0.0).astype(BF16)
        chunks = [zn[c * SGU_CHUNK:(c + 1) * SGU_CHUNK] for c in range(tm // SGU_CHUNK)]
        mixed = jnp.dot(wm, jnp.concatenate(chunks, axis=1), preferred_element_type=F32)
        mixed = [mixed[:, c * gd:(c + 1) * gd] + bs_ref[g] for c in range(len(chunks))]
        o = u * jnp.concatenate(mixed, axis=0)
        ssq = ssq + jnp.sum(o * o, axis=-1, keepdims=True)
        outs.append(o)
    scale = lax.rsqrt(ssq / sgu_w + EPS)
    for g in range(N_SGU_GROUPS):
        sgu_ref[:, g * gd:(g + 1) * gd] = (outs[g] * scale * gout_ref[:, g * gd:(g + 1) * gd]).astype(BF16)


def _cast_chunk_spec(w, steps):
    rows = w.shape[0]
    chunk = max(BF16_SUBLANES, -(-rows // steps // BF16_SUBLANES) * BF16_SUBLANES)
    while rows % chunk:
        chunk += BF16_SUBLANES
    last = rows // chunk - 1
    return pl.BlockSpec((chunk, w.shape[1]), lambda i: (jnp.minimum(i, last), 0))


def _proj_sgu(x2, g_pre, w_in, ln_g, ln_b, w_sp, b_sp, g_sgu, attn_w, sgu_w, to_cast):
    n, d = x2.shape
    tm = PROJ_TILE
    per_span = SPAN // tm
    cast_specs = [_cast_chunk_spec(w, n // tm) for w in to_cast]
    n_slabs = attn_w // LANES
    a_shape = jax.ShapeDtypeStruct((n_slabs, n // SPAN, 4, SPAN // 4, LANES), BF16)
    b_shape = jax.ShapeDtypeStruct((n_slabs, n // SPAN, 16, SPAN // 16, LANES), BF16)
    a_spec = pl.BlockSpec((n_slabs, 1, 4, tm // 4, LANES),
                          lambda i: (0, i // per_span, 0, i % per_span, 0))
    b_spec = pl.BlockSpec((n_slabs, 1, 16, tm // 16, LANES),
                          lambda i: (0, i // per_span, 0, i % per_span, 0))
    return pl.pallas_call(
        functools.partial(_proj_sgu_kernel, attn_w=attn_w, sgu_w=sgu_w, n_cast=len(to_cast)),
        grid=(n // tm,),
        in_specs=[
            pl.BlockSpec((tm, d), lambda i: (i, 0)),
            _const_spec(g_pre.shape), _const_spec(w_in.shape), _const_spec(ln_g.shape),
            _const_spec(ln_b.shape), _const_spec(w_sp.shape), _const_spec(b_sp.shape),
            _const_spec(g_sgu.shape),
        ] + cast_specs,
        out_specs=[a_spec] * 3 + [b_spec] * 3
        + [pl.BlockSpec((tm, sgu_w), lambda i: (i, 0))] + cast_specs,
        out_shape=[a_shape] * 3 + [b_shape] * 3
        + [jax.ShapeDtypeStruct((n, sgu_w), BF16)]
        + [jax.ShapeDtypeStruct(w.shape, BF16) for w in to_cast],
        scratch_shapes=[pltpu.VMEM((3, n_slabs, tm, LANES), F32),
                        pltpu.VMEM((3, n_slabs, 4, tm // 4, LANES), F32)],
        compiler_params=pltpu.CompilerParams(dimension_semantics=("arbitrary",),
                                             vmem_limit_bytes=VMEM_LIMIT),
        name="proj_sgu",
    )(x2, g_pre, w_in, ln_g, ln_b, w_sp, b_sp, g_sgu, *to_cast)


A_BLOCKS = SPAN // 4 // QBLK
PIECE = QBLK // 4


def _attn_kernel(slopes_ref, qa, ka, va, kap, vap, qb, kb, vb, kbp, vbp,
                 o_ref, bias_ref, ob, lb):
    slab = pl.program_id(1)
    span = pl.program_id(2)
    lane = lax.broadcasted_iota(jnp.int32, (QBLK, LANES), 1)
    head0 = lane < HEAD_DIM
    first = span == 0

    @pl.when(first)
    def _start_of_sequence():
        i = lax.broadcasted_iota(jnp.int32, (WIN, WIN), 0)
        j = lax.broadcasted_iota(jnp.int32, (WIN, WIN), 1)
        slope = jnp.where(i < QBLK, slopes_ref[HEADS_PER_SLAB * slab],
                          slopes_ref[HEADS_PER_SLAB * slab + 1])
        q_pos = i & (QBLK - 1)
        for b, d in enumerate(DILATIONS):
            if d == 1:
                q_at = 4 * (q_pos & (PIECE - 1)) + q_pos // PIECE
                k_at = 4 * (j & (2 * PIECE - 1)) + j // (2 * PIECE)
            else:
                q_at, k_at = q_pos, j
            steps = QBLK + q_at - k_at
            valid = (steps >= 0) & (steps <= QBLK)
            bias = jnp.where(valid, -(slope * LOG2E) * (steps * d).astype(F32), NEG)
            bias_ref[b, 1] = bias
            bias_ref[b, 0] = jnp.where(k_at >= QBLK, bias, NEG)

    not_first = jnp.where(first, 0, 1)
    ones = jnp.ones((WIN, LANES), BF16)

    def before_sequence(prev_blk):
        return jnp.where(first, jnp.zeros_like(prev_blk), prev_blk)

    def window(prev_blk, cur_ref, idx, nb):
        if nb > 0:
            return cur_ref[idx + (slice((nb - 1) * QBLK, (nb + 1) * QBLK), slice(None))]
        return jnp.concatenate([before_sequence(prev_blk),
                                cur_ref[idx + (slice(0, QBLK), slice(None))]], axis=0)

    def window1(prev_ref, cur_ref, nb):
        pieces = []
        for r in range(4):
            if nb > 0:
                pieces.append(cur_ref[0, 0, r, (nb - 1) * PIECE:(nb + 1) * PIECE, :])
            else:
                pieces += [before_sequence(prev_ref[0, 0, r, QBLK - PIECE:QBLK, :]),
                           cur_ref[0, 0, r, 0:PIECE, :]]
        return jnp.concatenate(pieces, axis=0)

    def block(b, q, kw, vw, nb):
        zero = jnp.zeros_like(q)
        qq = jnp.concatenate([jnp.where(head0, q, zero), jnp.where(head0, zero, q)], axis=0)
        s = lax.dot_general(qq, kw, (((1,), (1,)), ((), ())), preferred_element_type=F32)
        s = s + bias_ref[b, not_first if nb == 0 else 1]
        m = jnp.max(s, axis=1, keepdims=True)
        p = jnp.exp2(s - m).astype(BF16)
        acc = jnp.dot(p, jnp.concatenate([vw, ones], axis=1), preferred_element_type=F32)
        o_t = jnp.where(head0, acc[0:QBLK, 0:LANES], acc[QBLK:WIN, 0:LANES])
        l_t = jnp.where(head0, acc[0:QBLK, LANES:], acc[QBLK:WIN, LANES:])
        m_t = jnp.where(head0, m[0:QBLK], m[QBLK:WIN])
        return o_t / l_t, m_t + jnp.log2(l_t)

    for nb in range(SPAN // QBLK):
        q = jnp.concatenate([qa[0, 0, r, nb * PIECE:(nb + 1) * PIECE, :] for r in range(4)], axis=0)
        o_t, lse_t = block(0, q, window1(kap, ka, nb), window1(vap, va, nb), nb)
        for r in range(4):
            rows = slice(r * (SPAN // 4) + nb * PIECE, r * (SPAN // 4) + (nb + 1) * PIECE)
            ob[0, rows, :] = o_t[r * PIECE:(r + 1) * PIECE]
            lb[0, rows, :] = lse_t[r * PIECE:(r + 1) * PIECE]
    for r in range(4):
        for nb in range(A_BLOCKS):
            q = qa[0, 0, r, nb * QBLK:(nb + 1) * QBLK, :]
            o_t, lse_t = block(1, q, window(kap[0, 0, r], ka, (0, 0, r), nb),
                               window(vap[0, 0, r], va, (0, 0, r), nb), nb)
            rows = slice((r * A_BLOCKS + nb) * QBLK, (r * A_BLOCKS + nb + 1) * QBLK)
            ob[1, rows, :] = o_t
            lb[1, rows, :] = lse_t
    for r in range(16):
        o_t, lse_t = block(2, qb[0, 0, r], window(kbp[0, 0, r], kb, (0, 0, r), 0),
                           window(vbp[0, 0, r], vb, (0, 0, r), 0), 0)
        rows = pl.ds((r % 4) * (SPAN // 4) + r // 4, QBLK, stride=4)
        ob[2, rows, :] = o_t
        lb[2, rows, :] = lse_t

    for c in range(SPAN // WIN):
        rows = slice(c * WIN, (c + 1) * WIN)
        tokens = pl.ds(c // 2 + (c % 2) * (4 * WIN), WIN, stride=4)
        l0, l1, l2 = lb[0, rows, :], lb[1, rows, :], lb[2, rows, :]
        mx = jnp.maximum(jnp.maximum(l0, l1), l2)
        w0, w1, w2 = jnp.exp2(l0 - mx), jnp.exp2(l1 - mx), jnp.exp2(l2 - mx)
        num = w0 * ob[0, rows, :] + w1 * ob[1, rows, :] + w2 * ob[2, rows, :]
        o_ref[0, tokens, :] = num / (w0 + w1 + w2)


def _attention(qkv_layouts, slopes, batch, seq):
    qa, ka, va, qb, kb, vb = qkv_layouts
    n_slabs = qa.shape[0]
    n = batch * seq
    assert seq % SPAN == 0
    spans = seq // SPAN

    def cur(block_shape):
        zeros = (0,) * (len(block_shape) - 2)
        return pl.BlockSpec(block_shape, lambda b, s, t: (s, b * spans + t) + zeros)

    a_spec = cur((1, 1, 4, SPAN // 4, LANES))
    b_spec = cur((1, 1, 16, QBLK, LANES))
    ap_spec = pl.BlockSpec((1, 1, 4, QBLK, LANES),
                           lambda b, s, t: (s, jnp.maximum(b * spans + t - 1, 0), 0, A_BLOCKS - 1, 0))
    bp_spec = pl.BlockSpec((1, 1, 16, QBLK, LANES),
                           lambda b, s, t: (s, jnp.maximum(b * spans + t - 1, 0), 0, 0, 0))
    return pl.pallas_call(
        _attn_kernel,
        grid=(batch, n_slabs, spans),
        in_specs=[pl.BlockSpec(memory_space=pltpu.SMEM),
                  a_spec, a_spec, a_spec, ap_spec, ap_spec,
                  b_spec, b_spec, b_spec, bp_spec, bp_spec],
        out_specs=pl.BlockSpec((1, SPAN, LANES), lambda b, s, t: (s, b * spans + t, 0)),
        out_shape=jax.ShapeDtypeStruct((n_slabs, n, LANES), F32),
        scratch_shapes=[
            pltpu.VMEM((len(DILATIONS), 2, WIN, WIN), F32),
            pltpu.VMEM((len(DILATIONS), SPAN, LANES), F32),
            pltpu.VMEM((len(DILATIONS), SPAN, LANES), F32),
        ],
        compiler_params=pltpu.CompilerParams(
            dimension_semantics=("arbitrary", "arbitrary", "arbitrary"),
            vmem_limit_bytes=VMEM_LIMIT),
        name="attn",
    )(slopes, qa, ka, va, ka, va, qb, kb, vb, kb, vb)


def _tail_kernel(x_ref, attn_ref, sgu_ref, p_ref, ga_ref, wout_ref, gpm_ref, gpf_ref,
                 wgu_ref, wd_ref, gpo_ref, wpg_ref, bpg_ref, wpp_ref, o_ref):
    d_ff = wd_ref.shape[0]
    bounds = (0,) + TAIL_SPLITS + (x_ref.shape[0],)
    chains = [slice(lo, hi) for lo, hi in zip(bounds[:-1], bounds[1:])]

    def dot(a, w_ref):
        return jnp.dot(a, w_ref[...], preferred_element_type=F32)

    pe = [dot(p_ref[rows, :].astype(BF16), wpp_ref) for rows in chains]
    groups = []
    for rows in chains:
        attn = jnp.concatenate([attn_ref[s, rows, :] for s in range(attn_ref.shape[0])], axis=-1)
        an = _rms(attn, ga_ref[...]).astype(BF16)
        groups.append(jnp.concatenate([an, sgu_ref[rows, :]], axis=-1))
    mixed = [dot(g, wout_ref) for g in groups]
    h = [x_ref[rows, :] + _rms(m, gpm_ref[...]) for rows, m in zip(chains, mixed)]
    f = [_rms(hc, gpf_ref[...]).astype(BF16) for hc in h]
    gu = [dot(fc, wgu_ref) for fc in f]
    hid = [(jax.nn.silu(g[:, :d_ff]) * g[:, d_ff:]).astype(BF16) for g in gu]
    y = [dot(hc, wd_ref) for hc in hid]
    h = [hc + _rms(yc, gpo_ref[...]) for hc, yc in zip(h, y)]
    gate = [jax.nn.sigmoid(dot(hc.astype(BF16), wpg_ref) + bpg_ref[...]) for hc in h]
    for rows, hc, gc, pc in zip(chains, h, gate, pe):
        o_ref[rows, :] = hc + gc * pc


def _tail(x2, attn, sgu, p2, *params):
    n, d = x2.shape
    tm = TAIL_TILE

    def tile(a):
        return pl.BlockSpec((tm, a.shape[1]), lambda i: (i, 0))

    return pl.pallas_call(
        _tail_kernel,
        grid=(n // tm,),
        in_specs=[tile(x2), pl.BlockSpec((attn.shape[0], tm, LANES), lambda i: (0, i, 0)),
                  tile(sgu), tile(p2)] + [_const_spec(a.shape) for a in params],
        out_specs=pl.BlockSpec((tm, d), lambda i: (i, 0)),
        out_shape=jax.ShapeDtypeStruct((n, d), F32),
        compiler_params=pltpu.CompilerParams(dimension_semantics=("arbitrary",),
                                             vmem_limit_bytes=VMEM_LIMIT),
        name="tail",
    )(x2, attn, sgu, p2, *params)


def _group_major(w_in, attn_w, sgu_w):
    gd = sgu_w // N_SGU_GROUPS
    u0 = 3 * attn_w
    z0 = u0 + sgu_w
    cols = []
    for g in range(N_SGU_GROUPS):
        cols += [w_in[:, u0 + g * gd:u0 + (g + 1) * gd], w_in[:, z0 + g * gd:z0 + (g + 1) * gd]]
    return jnp.concatenate(cols + [w_in[:, :u0]], axis=1)


def kernel(x, p, ln_pre_mix, w_in, sgu_ln_g, sgu_ln_b, w_spatial, b_spatial, attn_out_norm,
           sgu_out_norm, w_out, ln_post_mix, ln_pre_ffn, w_gate_up, w_down, ln_post_ffn,
           w_pe_gate, b_pe_gate, w_pe_proj):
    batch, seq, d = x.shape
    depth = w_in.shape[0]
    attn_w = attn_out_norm.shape[1]
    sgu_w = sgu_out_norm.shape[1]
    n_heads = attn_w // HEAD_DIM
    slopes = 2.0 ** (-8.0 * (jnp.arange(n_heads, dtype=F32) + 1.0) / n_heads)

    def row(a):
        return a.reshape(1, -1)

    h = x.reshape(batch * seq, d)
    for i in range(depth):
        tail_weights = (w_out[i], w_gate_up[i], w_down[i], w_pe_gate[i], w_pe_proj[i])
        outs = _proj_sgu(
            h, row(ln_pre_mix[i]), _group_major(w_in[i], attn_w, sgu_w).astype(BF16),
            row(sgu_ln_g[i]), row(sgu_ln_b[i]),
            w_spatial[i], b_spatial[i][:, :, None], row(sgu_out_norm[i]), attn_w, sgu_w,
            tail_weights)
        qkv_layouts, sgu = outs[:6], outs[6]
        wout_b, wgu_b, wd_b, wpg_b, wpp_b = outs[7:]
        attn = _attention(qkv_layouts, slopes, batch, seq)
        h = _tail(h, attn, sgu, p[i].reshape(batch * seq, -1),
                  row(attn_out_norm[i]), wout_b, row(ln_post_mix[i]),
                  row(ln_pre_ffn[i]), wgu_b, wd_b,
                  row(ln_post_ffn[i]), wpg_b, row(b_pe_gate[i]), wpp_b)
    return h.reshape(batch, seq, d)
```

```python
import functools

import jax
import jax.numpy as jnp
from jax import lax
from jax.experimental import pallas as pl
from jax.experimental.pallas import tpu as pltpu

F32 = jnp.float32
BF16 = jnp.bfloat16

HEAD_DIM = 64
QBLK = 128
WIN = 2 * QBLK
DILATIONS = (1, 4, 16)
SPAN = QBLK * DILATIONS[-1]
SGU_CHUNK = 128
N_SGU_GROUPS = 4
EPS = 1e-6
NEG = -1e30
LOG2E = 1.4426950408889634
Q_SCALE = HEAD_DIM ** -0.5 * LOG2E

LANES = 128
BF16_SUBLANES = 16
HEADS_PER_SLAB = LANES // HEAD_DIM

PROJ_TILE = 512
TAIL_TILE = 512
TAIL_SPLITS = (256,)
VMEM_LIMIT = 56 * 1024 * 1024


def _rms(x, g):
    return x * lax.rsqrt(jnp.mean(x * x, axis=-1, keepdims=True) + EPS) * g


def _const_spec(shape):
    zeros = (0,) * len(shape)
    return pl.BlockSpec(shape, lambda *_: zeros, pipeline_mode=pl.Buffered(1))


def _proj_sgu_kernel(*refs, attn_w, sgu_w, n_cast):
    x_ref, g_ref, w_ref, lng_ref, lnb_ref, ws_ref, bs_ref, gout_ref = refs[:8]
    cast_in = refs[8:8 + n_cast]
    qa, ka, va, qb, kb, vb, sgu_ref = refs[8 + n_cast:15 + n_cast]
    cast_out = refs[15 + n_cast:15 + 2 * n_cast]
    tok_tmp, res_tmp = refs[15 + 2 * n_cast:]
    for src, dst in zip(cast_in, cast_out):
        dst[...] = src[...].astype(dst.dtype)
    tm = x_ref.shape[0]
    gd = sgu_w // N_SGU_GROUPS
    a = _rms(x_ref[...], g_ref[...]).astype(BF16)

    def project(c0, width):
        return jnp.dot(a, w_ref[:, c0:c0 + width], preferred_element_type=F32)

    def emit_layouts(idx, a_ref, b_ref, cols):
        for s in range(attn_w // LANES):
            tok_tmp[idx, s] = cols[:, s * LANES:(s + 1) * LANES]
            for r in range(4):
                every4 = tok_tmp[idx, s, pl.ds(r, tm // 4, stride=4), :]
                a_ref[s, 0, r] = every4.astype(BF16)
                res_tmp[idx, s, r] = every4
                for r2 in range(4):
                    b_ref[s, 0, 4 * r2 + r] = (
                        res_tmp[idx, s, r, pl.ds(r2, tm // 16, stride=4), :].astype(BF16))

    row = lax.broadcasted_iota(jnp.int32, (SGU_CHUNK, SGU_CHUNK), 0)
    col = lax.broadcasted_iota(jnp.int32, (SGU_CHUNK, SGU_CHUNK), 1)
    causal = row >= col
    def gate(g, uz):
        u = jax.nn.gelu(uz[:, :gd])
        z = jax.nn.gelu(uz[:, gd:])
        zc = z - jnp.mean(z, axis=-1, keepdims=True)
        zn = zc * lax.rsqrt(jnp.mean(zc * zc, axis=-1, keepdims=True) + EPS)
        zn = (zn * lng_ref[...] + lnb_ref[...]).astype(BF16)
        wm = jnp.where(causal, ws_ref[g], 0.0).astype(BF16)
        chunks = [zn[c * SGU_CHUNK:(c + 1) * SGU_CHUNK] for c in range(tm // SGU_CHUNK)]
        mixed = jnp.dot(wm, jnp.concatenate(chunks, axis=1), preferred_element_type=F32)
        mixed = [mixed[:, c * gd:(c + 1) * gd] + bs_ref[g] for c in range(len(chunks))]
        return u * jnp.concatenate(mixed, axis=0)

    uz01 = project(0, 4 * gd)
    uz23 = project(4 * gd, 4 * gd)
    outs = [gate(0, uz01[:, :2 * gd])]
    emit_layouts(0, qa, qb, project(2 * sgu_w, attn_w) * Q_SCALE)
    outs.append(gate(1, uz01[:, 2 * gd:]))
    emit_layouts(1, ka, kb, project(2 * sgu_w + attn_w, attn_w))
    outs.append(gate(2, uz23[:, :2 * gd]))
    outs.append(gate(3, uz23[:, 2 * gd:]))
    ssq = sum(jnp.sum(o * o, axis=-1, keepdims=True) for o in outs)
    scale = lax.rsqrt(ssq / sgu_w + EPS)
    for g in range(N_SGU_GROUPS):
        sgu_ref[:, g * gd:(g + 1) * gd] = (outs[g] * scale * gout_ref[:, g * gd:(g + 1) * gd]).astype(BF16)
    emit_layouts(2, va, vb, project(2 * sgu_w + 2 * attn_w, attn_w))


def _cast_chunk_spec(w, steps):
    rows = w.shape[0]
    chunk = max(BF16_SUBLANES, -(-rows // steps // BF16_SUBLANES) * BF16_SUBLANES)
    while rows % chunk:
        chunk += BF16_SUBLANES
    last = rows // chunk - 1
    return pl.BlockSpec((chunk, w.shape[1]), lambda i: (jnp.minimum(i, last), 0))


def _proj_sgu(x2, g_pre, w_in, ln_g, ln_b, w_sp, b_sp, g_sgu, attn_w, sgu_w, to_cast):
    n, d = x2.shape
    tm = PROJ_TILE
    per_span = SPAN // tm
    cast_specs = [_cast_chunk_spec(w, n // tm) for w in to_cast]
    n_slabs = attn_w // LANES
    a_shape = jax.ShapeDtypeStruct((n_slabs, n // SPAN, 4, SPAN // 4, LANES), BF16)
    b_shape = jax.ShapeDtypeStruct((n_slabs, n // SPAN, 16, SPAN // 16, LANES), BF16)
    a_spec = pl.BlockSpec((n_slabs, 1, 4, tm // 4, LANES),
                          lambda i: (0, i // per_span, 0, i % per_span, 0))
    b_spec = pl.BlockSpec((n_slabs, 1, 16, tm // 16, LANES),
                          lambda i: (0, i // per_span, 0, i % per_span, 0))
    return pl.pallas_call(
        functools.partial(_proj_sgu_kernel, attn_w=attn_w, sgu_w=sgu_w, n_cast=len(to_cast)),
        grid=(n // tm,),
        in_specs=[
            pl.BlockSpec((tm, d), lambda i: (i, 0)),
            _const_spec(g_pre.shape), _const_spec(w_in.shape), _const_spec(ln_g.shape),
            _const_spec(ln_b.shape), _const_spec(w_sp.shape), _const_spec(b_sp.shape),
            _const_spec(g_sgu.shape),
        ] + cast_specs,
        out_specs=[a_spec] * 3 + [b_spec] * 3
        + [pl.BlockSpec((tm, sgu_w), lambda i: (i, 0))] + cast_specs,
        out_shape=[a_shape] * 3 + [b_shape] * 3
        + [jax.ShapeDtypeStruct((n, sgu_w), BF16)]
        + [jax.ShapeDtypeStruct(w.shape, BF16) for w in to_cast],
        scratch_shapes=[pltpu.VMEM((3, n_slabs, tm, LANES), F32),
                        pltpu.VMEM((3, n_slabs, 4, tm // 4, LANES), F32)],
        compiler_params=pltpu.CompilerParams(dimension_semantics=("arbitrary",),
                                             vmem_limit_bytes=VMEM_LIMIT),
        name="proj_sgu",
    )(x2, g_pre, w_in, ln_g, ln_b, w_sp, b_sp, g_sgu, *to_cast)


A_BLOCKS = SPAN // 4 // QBLK
PIECE = QBLK // 4


def _attn_kernel(slopes_ref, qa, ka, va, kap, vap, qb, kb, vb, kbp, vbp,
                 o_ref, bias_ref, ob, lb):
    slab = pl.program_id(1)
    span = pl.program_id(2)
    lane = lax.broadcasted_iota(jnp.int32, (QBLK, LANES), 1)
    head0 = lane < HEAD_DIM
    first = span == 0

    @pl.when(first)
    def _start_of_sequence():
        i = lax.broadcasted_iota(jnp.int32, (WIN, WIN), 0)
        j = lax.broadcasted_iota(jnp.int32, (WIN, WIN), 1)
        slope = jnp.where(i < QBLK, slopes_ref[HEADS_PER_SLAB * slab],
                          slopes_ref[HEADS_PER_SLAB * slab + 1])
        q_pos = i & (QBLK - 1)
        for b, d in enumerate(DILATIONS):
            if d == 1:
                q_at = 4 * (q_pos & (PIECE - 1)) + q_pos // PIECE
                k_at = 4 * (j & (2 * PIECE - 1)) + j // (2 * PIECE)
            else:
                q_at, k_at = q_pos, j
            steps = QBLK + q_at - k_at
            valid = (steps >= 0) & (steps <= QBLK)
            bias = jnp.where(valid, -(slope * LOG2E) * (steps * d).astype(F32), NEG)
            bias_ref[b, 1] = bias
            bias_ref[b, 0] = jnp.where(k_at >= QBLK, bias, NEG)

    not_first = jnp.where(first, 0, 1)
    ones = jnp.ones((WIN, LANES), BF16)

    def before_sequence(prev_blk):
        return jnp.where(first, jnp.zeros_like(prev_blk), prev_blk)

    def window(prev_blk, cur_ref, idx, nb):
        if nb > 0:
            return cur_ref[idx + (slice((nb - 1) * QBLK, (nb + 1) * QBLK), slice(None))]
        return jnp.concatenate([before_sequence(prev_blk),
                                cur_ref[idx + (slice(0, QBLK), slice(None))]], axis=0)

    def window1(prev_ref, cur_ref, nb):
        pieces = []
        for r in range(4):
            if nb > 0:
                pieces.append(cur_ref[0, 0, r, (nb - 1) * PIECE:(nb + 1) * PIECE, :])
            else:
                pieces += [before_sequence(prev_ref[0, 0, r, QBLK - PIECE:QBLK, :]),
                           cur_ref[0, 0, r, 0:PIECE, :]]
        return jnp.concatenate(pieces, axis=0)

    def block(b, q, kw, vw, nb):
        zero = jnp.zeros_like(q)
        qq = jnp.concatenate([jnp.where(head0, q, zero), jnp.where(head0, zero, q)], axis=0)
        s = lax.dot_general(qq, kw, (((1,), (1,)), ((), ())), preferred_element_type=F32)
        s = s + bias_ref[b, not_first if nb == 0 else 1]
        m = jnp.max(s, axis=1, keepdims=True)
        p = jnp.exp2(s - m).astype(BF16)
        acc = jnp.dot(p, jnp.concatenate([vw, ones], axis=1), preferred_element_type=F32)
        o_t = jnp.where(head0, acc[0:QBLK, 0:LANES], acc[QBLK:WIN, 0:LANES])
        l_t = jnp.where(head0, acc[0:QBLK, LANES:], acc[QBLK:WIN, LANES:])
        m_t = jnp.where(head0, m[0:QBLK], m[QBLK:WIN])
        return o_t / l_t, m_t + jnp.log2(l_t)

    for nb in range(SPAN // QBLK):
        q = jnp.concatenate([qa[0, 0, r, nb * PIECE:(nb + 1) * PIECE, :] for r in range(4)], axis=0)
        o_t, lse_t = block(0, q, window1(kap, ka, nb), window1(vap, va, nb), nb)
        for r in range(4):
            rows = slice(r * (SPAN // 4) + nb * PIECE, r * (SPAN // 4) + (nb + 1) * PIECE)
            ob[0, rows, :] = o_t[r * PIECE:(r + 1) * PIECE]
            lb[0, rows, :] = lse_t[r * PIECE:(r + 1) * PIECE]
    for r in range(4):
        for nb in range(A_BLOCKS):
            q = qa[0, 0, r, nb * QBLK:(nb + 1) * QBLK, :]
            o_t, lse_t = block(1, q, window(kap[0, 0, r], ka, (0, 0, r), nb),
                               window(vap[0, 0, r], va, (0, 0, r), nb), nb)
            rows = slice((r * A_BLOCKS + nb) * QBLK, (r * A_BLOCKS + nb + 1) * QBLK)
            ob[1, rows, :] = o_t
            lb[1, rows, :] = lse_t
    for r in range(16):
        o_t, lse_t = block(2, qb[0, 0, r], window(kbp[0, 0, r], kb, (0, 0, r), 0),
                           window(vbp[0, 0, r], vb, (0, 0, r), 0), 0)
        rows = pl.ds((r % 4) * (SPAN // 4) + r // 4, QBLK, stride=4)
        ob[2, rows, :] = o_t
        lb[2, rows, :] = lse_t

    for c in range(SPAN // WIN):
        rows = slice(c * WIN, (c + 1) * WIN)
        tokens = pl.ds(c // 2 + (c % 2) * (4 * WIN), WIN, stride=4)
        l0, l1, l2 = lb[0, rows, :], lb[1, rows, :], lb[2, rows, :]
        mx = jnp.maximum(jnp.maximum(l0, l1), l2)
        w0, w1, w2 = jnp.exp2(l0 - mx), jnp.exp2(l1 - mx), jnp.exp2(l2 - mx)
        num = w0 * ob[0, rows, :] + w1 * ob[1, rows, :] + w2 * ob[2, rows, :]
        o_ref[0, tokens, :] = num / (w0 + w1 + w2)


def _attention(qkv_layouts, slopes, batch, seq):
    qa, ka, va, qb, kb, vb = qkv_layouts
    n_slabs = qa.shape[0]
    n = batch * seq
    assert seq % SPAN == 0
    spans = seq // SPAN

    def cur(block_shape):
        zeros = (0,) * (len(block_shape) - 2)
        return pl.BlockSpec(block_shape, lambda b, s, t: (s, b * spans + t) + zeros)

    a_spec = cur((1, 1, 4, SPAN // 4, LANES))
    b_spec = cur((1, 1, 16, QBLK, LANES))
    ap_spec = pl.BlockSpec((1, 1, 4, QBLK, LANES),
                           lambda b, s, t: (s, jnp.maximum(b * spans + t - 1, 0), 0, A_BLOCKS - 1, 0))
    bp_spec = pl.BlockSpec((1, 1, 16, QBLK, LANES),
                           lambda b, s, t: (s, jnp.maximum(b * spans + t - 1, 0), 0, 0, 0))
    return pl.pallas_call(
        _attn_kernel,
        grid=(batch, n_slabs, spans),
        in_specs=[pl.BlockSpec(memory_space=pltpu.SMEM),
                  a_spec, a_spec, a_spec, ap_spec, ap_spec,
                  b_spec, b_spec, b_spec, bp_spec, bp_spec],
        out_specs=pl.BlockSpec((1, SPAN, LANES), lambda b, s, t: (s, b * spans + t, 0)),
        out_shape=jax.ShapeDtypeStruct((n_slabs, n, LANES), F32),
        scratch_shapes=[
            pltpu.VMEM((len(DILATIONS), 2, WIN, WIN), F32),
            pltpu.VMEM((len(DILATIONS), SPAN, LANES), F32),
            pltpu.VMEM((len(DILATIONS), SPAN, LANES), F32),
        ],
        compiler_params=pltpu.CompilerParams(
            dimension_semantics=("arbitrary", "arbitrary", "arbitrary"),
            vmem_limit_bytes=VMEM_LIMIT),
        name="attn",
    )(slopes, qa, ka, va, ka, va, qb, kb, vb, kb, vb)


def _tail_kernel(x_ref, attn_ref, sgu_ref, p_ref, ga_ref, wout_ref, gpm_ref, gpf_ref,
                 wgu_ref, wd_ref, gpo_ref, wpg_ref, bpg_ref, wpp_ref, o_ref):
    d_ff = wd_ref.shape[0]
    bounds = (0,) + TAIL_SPLITS + (x_ref.shape[0],)
    chains = [slice(lo, hi) for lo, hi in zip(bounds[:-1], bounds[1:])]

    def dot(a, w_ref):
        return jnp.dot(a, w_ref[...], preferred_element_type=F32)

    pe = [dot(p_ref[rows, :].astype(BF16), wpp_ref) for rows in chains]
    groups = []
    for rows in chains:
        attn = jnp.concatenate([attn_ref[s, rows, :] for s in range(attn_ref.shape[0])], axis=-1)
        an = _rms(attn, ga_ref[...]).astype(BF16)
        groups.append(jnp.concatenate([an, sgu_ref[rows, :]], axis=-1))
    mixed = [dot(g, wout_ref) for g in groups]
    h = [x_ref[rows, :] + _rms(m, gpm_ref[...]) for rows, m in zip(chains, mixed)]
    f = [_rms(hc, gpf_ref[...]).astype(BF16) for hc in h]
    gu = [dot(fc, wgu_ref) for fc in f]
    hid = [(jax.nn.silu(g[:, :d_ff]) * g[:, d_ff:]).astype(BF16) for g in gu]
    y = [dot(hc, wd_ref) for hc in hid]
    h = [hc + _rms(yc, gpo_ref[...]) for hc, yc in zip(h, y)]
    gate = [jax.nn.sigmoid(dot(hc.astype(BF16), wpg_ref) + bpg_ref[...]) for hc in h]
    for rows, hc, gc, pc in zip(chains, h, gate, pe):
        o_ref[rows, :] = hc + gc * pc


def _tail(x2, attn, sgu, p2, *params):
    n, d = x2.shape
    tm = TAIL_TILE

    def tile(a):
        return pl.BlockSpec((tm, a.shape[1]), lambda i: (i, 0))

    return pl.pallas_call(
        _tail_kernel,
        grid=(n // tm,),
        in_specs=[tile(x2), pl.BlockSpec((attn.shape[0], tm, LANES), lambda i: (0, i, 0)),
                  tile(sgu), tile(p2)] + [_const_spec(a.shape) for a in params],
        out_specs=pl.BlockSpec((tm, d), lambda i: (i, 0)),
        out_shape=jax.ShapeDtypeStruct((n, d), F32),
        compiler_params=pltpu.CompilerParams(dimension_semantics=("arbitrary",),
                                             vmem_limit_bytes=VMEM_LIMIT),
        name="tail",
    )(x2, attn, sgu, p2, *params)


def _group_major(w_in, attn_w, sgu_w):
    gd = sgu_w // N_SGU_GROUPS
    u0 = 3 * attn_w
    z0 = u0 + sgu_w
    cols = []
    for g in range(N_SGU_GROUPS):
        cols += [w_in[:, u0 + g * gd:u0 + (g + 1) * gd], w_in[:, z0 + g * gd:z0 + (g + 1) * gd]]
    return jnp.concatenate(cols + [w_in[:, :u0]], axis=1)


def kernel(x, p, ln_pre_mix, w_in, sgu_ln_g, sgu_ln_b, w_spatial, b_spatial, attn_out_norm,
           sgu_out_norm, w_out, ln_post_mix, ln_pre_ffn, w_gate_up, w_down, ln_post_ffn,
           w_pe_gate, b_pe_gate, w_pe_proj):
    batch, seq, d = x.shape
    depth = w_in.shape[0]
    attn_w = attn_out_norm.shape[1]
    sgu_w = sgu_out_norm.shape[1]
    n_heads = attn_w // HEAD_DIM
    slopes = 2.0 ** (-8.0 * (jnp.arange(n_heads, dtype=F32) + 1.0) / n_heads)

    def row(a):
        return a.reshape(1, -1)

    h = x.reshape(batch * seq, d)
    for i in range(depth):
        tail_weights = (w_out[i], w_gate_up[i], w_down[i], w_pe_gate[i], w_pe_proj[i])
        outs = _proj_sgu(
            h, row(ln_pre_mix[i]), _group_major(w_in[i], attn_w, sgu_w).astype(BF16),
            row(sgu_ln_g[i]), row(sgu_ln_b[i]),
            w_spatial[i], b_spatial[i][:, :, None], row(sgu_out_norm[i]), attn_w, sgu_w,
            tail_weights)
        qkv_layouts, sgu = outs[:6], outs[6]
        wout_b, wgu_b, wd_b, wpg_b, wpp_b = outs[7:]
        attn = _attention(qkv_layouts, slopes, batch, seq)
        h = _tail(h, attn, sgu, p[i].reshape(batch * seq, -1),
                  row(attn_out_norm[i]), wout_b, row(ln_post_mix[i]),
                  row(ln_pre_ffn[i]), wgu_b, wd_b,
                  row(ln_post_ffn[i]), wpg_b, row(b_pe_gate[i]), wpp_b)
    return h.reshape(batch, seq, d)
```

```python
import functools

import jax
import jax.numpy as jnp
from jax import lax
from jax.experimental import pallas as pl
from jax.experimental.pallas import tpu as pltpu

F32 = jnp.float32
BF16 = jnp.bfloat16

HEAD_DIM = 64
QBLK = 128
WIN = 2 * QBLK
DILATIONS = (1, 4, 16)
SPAN = QBLK * DILATIONS[-1]
SGU_CHUNK = 128
N_SGU_GROUPS = 4
EPS = 1e-6
NEG = -1e30
LOG2E = 1.4426950408889634
Q_SCALE = HEAD_DIM ** -0.5 * LOG2E

LANES = 128
BF16_SUBLANES = 16
HEADS_PER_SLAB = LANES // HEAD_DIM

PROJ_TILE = 512
TAIL_TILE = 512
TAIL_SPLITS = (256,)
VMEM_LIMIT = 56 * 1024 * 1024


def _rms(x, g):
    return x * lax.rsqrt(jnp.mean(x * x, axis=-1, keepdims=True) + EPS) * g


def _const_spec(shape):
    zeros = (0,) * len(shape)
    return pl.BlockSpec(shape, lambda *_: zeros, pipeline_mode=pl.Buffered(1))


def _proj_sgu_kernel(*refs, attn_w, sgu_w, n_cast):
    x_ref, g_ref, w_ref, lng_ref, lnb_ref, ws_ref, bs_ref, gout_ref = refs[:8]
    cast_in = refs[8:8 + n_cast]
    qa, ka, va, qb, kb, vb, sgu_ref = refs[8 + n_cast:15 + n_cast]
    cast_out = refs[15 + n_cast:15 + 2 * n_cast]
    tok_tmp, res_tmp = refs[15 + 2 * n_cast:]
    for src, dst in zip(cast_in, cast_out):
        dst[...] = src[...].astype(dst.dtype)
    tm = x_ref.shape[0]
    gd = sgu_w // N_SGU_GROUPS
    a = _rms(x_ref[...], g_ref[...]).astype(BF16)

    def project(c0, width):
        return jnp.dot(a, w_ref[:, c0:c0 + width], preferred_element_type=F32)

    def emit_layouts(idx, a_ref, b_ref, cols):
        for s in range(attn_w // LANES):
            tok_tmp[idx, s] = cols[:, s * LANES:(s + 1) * LANES]
            for r in range(4):
                every4 = tok_tmp[idx, s, pl.ds(r, tm // 4, stride=4), :]
                a_ref[s, 0, r] = every4.astype(BF16)
                res_tmp[idx, s, r] = every4
                for r2 in range(4):
                    b_ref[s, 0, 4 * r2 + r] = (
                        res_tmp[idx, s, r, pl.ds(r2, tm // 16, stride=4), :].astype(BF16))

    row = lax.broadcasted_iota(jnp.int32, (SGU_CHUNK, SGU_CHUNK), 0)
    col = lax.broadcasted_iota(jnp.int32, (SGU_CHUNK, SGU_CHUNK), 1)
    causal = row >= col
    def gate(g, uz):
        u = jax.nn.gelu(uz[:, :gd])
        z = jax.nn.gelu(uz[:, gd:])
        zc = z - jnp.mean(z, axis=-1, keepdims=True)
        zn = zc * lax.rsqrt(jnp.mean(zc * zc, axis=-1, keepdims=True) + EPS)
        zn = (zn * lng_ref[...] + lnb_ref[...]).astype(BF16)
        wm = jnp.where(causal, ws_ref[g], 0.0).astype(BF16)
        chunks = [zn[c * SGU_CHUNK:(c + 1) * SGU_CHUNK] for c in range(tm // SGU_CHUNK)]
        mixed = jnp.dot(wm, jnp.concatenate(chunks, axis=1), preferred_element_type=F32)
        mixed = [mixed[:, c * gd:(c + 1) * gd] + bs_ref[g] for c in range(len(chunks))]
        return u * jnp.concatenate(mixed, axis=0)

    uz01 = project(0, 4 * gd)
    uz23 = project(4 * gd, 4 * gd)
    outs = [gate(0, uz01[:, :2 * gd])]
    emit_layouts(0, qa, qb, project(2 * sgu_w, attn_w) * Q_SCALE)
    outs.append(gate(1, uz01[:, 2 * gd:]))
    emit_layouts(1, ka, kb, project(2 * sgu_w + attn_w, attn_w))
    outs.append(gate(2, uz23[:, :2 * gd]))
    outs.append(gate(3, uz23[:, 2 * gd:]))
    ssq = sum(jnp.sum(o * o, axis=-1, keepdims=True) for o in outs)
    scale = lax.rsqrt(ssq / sgu_w + EPS)
    for g in range(N_SGU_GROUPS):
        sgu_ref[:, g * gd:(g + 1) * gd] = (outs[g] * scale * gout_ref[:, g * gd:(g + 1) * gd]).astype(BF16)
    emit_layouts(2, va, vb, project(2 * sgu_w + 2 * attn_w, attn_w))


def _cast_chunk_spec(w, steps):
    rows = w.shape[0]
    chunk = max(BF16_SUBLANES, -(-rows // steps // BF16_SUBLANES) * BF16_SUBLANES)
    while rows % chunk:
        chunk += BF16_SUBLANES
    last = rows // chunk - 1
    return pl.BlockSpec((chunk, w.shape[1]), lambda i: (jnp.minimum(i, last), 0))


def _proj_sgu(x2, g_pre, w_in, ln_g, ln_b, w_sp, b_sp, g_sgu, attn_w, sgu_w, to_cast):
    n, d = x2.shape
    tm = PROJ_TILE
    per_span = SPAN // tm
    cast_specs = [_cast_chunk_spec(w, n // tm) for w in to_cast]
    n_slabs = attn_w // LANES
    a_shape = jax.ShapeDtypeStruct((n_slabs, n // SPAN, 4, SPAN // 4, LANES), BF16)
    b_shape = jax.ShapeDtypeStruct((n_slabs, n // SPAN, 16, SPAN // 16, LANES), BF16)
    a_spec = pl.BlockSpec((n_slabs, 1, 4, tm // 4, LANES),
                          lambda i: (0, i // per_span, 0, i % per_span, 0))
    b_spec = pl.BlockSpec((n_slabs, 1, 16, tm // 16, LANES),
                          lambda i: (0, i // per_span, 0, i % per_span, 0))
    return pl.pallas_call(
        functools.partial(_proj_sgu_kernel, attn_w=attn_w, sgu_w=sgu_w, n_cast=len(to_cast)),
        grid=(n // tm,),
        in_specs=[
            pl.BlockSpec((tm, d), lambda i: (i, 0)),
            _const_spec(g_pre.shape), _const_spec(w_in.shape), _const_spec(ln_g.shape),
            _const_spec(ln_b.shape), _const_spec(w_sp.shape), _const_spec(b_sp.shape),
            _const_spec(g_sgu.shape),
        ] + cast_specs,
        out_specs=[a_spec] * 3 + [b_spec] * 3
        + [pl.BlockSpec((tm, sgu_w), lambda i: (i, 0))] + cast_specs,
        out_shape=[a_shape] * 3 + [b_shape] * 3
        + [jax.ShapeDtypeStruct((n, sgu_w), BF16)]
        + [jax.ShapeDtypeStruct(w.shape, BF16) for w in to_cast],
        scratch_shapes=[pltpu.VMEM((3, n_slabs, tm, LANES), F32),
                        pltpu.VMEM((3, n_slabs, 4, tm // 4, LANES), F32)],
        compiler_params=pltpu.CompilerParams(dimension_semantics=("arbitrary",),
                                             vmem_limit_bytes=VMEM_LIMIT),
        name="proj_sgu",
    )(x2, g_pre, w_in, ln_g, ln_b, w_sp, b_sp, g_sgu, *to_cast)


A_BLOCKS = SPAN // 4 // QBLK
PIECE = QBLK // 4


def _attn_kernel(slopes_ref, qa, ka, va, kap, vap, qb, kb, vb, kbp, vbp,
                 o_ref, bias_ref, ob, lb):
    slab = pl.program_id(1)
    span = pl.program_id(2)
    lane = lax.broadcasted_iota(jnp.int32, (QBLK, LANES), 1)
    head0 = lane < HEAD_DIM
    first = span == 0

    @pl.when(first)
    def _start_of_sequence():
        i = lax.broadcasted_iota(jnp.int32, (WIN, WIN), 0)
        j = lax.broadcasted_iota(jnp.int32, (WIN, WIN), 1)
        slope = jnp.where(i < QBLK, slopes_ref[HEADS_PER_SLAB * slab],
                          slopes_ref[HEADS_PER_SLAB * slab + 1])
        q_pos = i & (QBLK - 1)
        for b, d in enumerate(DILATIONS):
            if d == 1:
                q_at = 4 * (q_pos & (PIECE - 1)) + q_pos // PIECE
                k_at = 4 * (j & (2 * PIECE - 1)) + j // (2 * PIECE)
            else:
                q_at, k_at = q_pos, j
            steps = QBLK + q_at - k_at
            valid = (steps >= 0) & (steps <= QBLK)
            bias = jnp.where(valid, -(slope * LOG2E) * (steps * d).astype(F32), NEG)
            bias_ref[b, 1] = bias
            bias_ref[b, 0] = jnp.where(k_at >= QBLK, bias, NEG)

    not_first = jnp.where(first, 0, 1)
    ones = jnp.ones((WIN, LANES), BF16)

    def before_sequence(prev_blk):
        return jnp.where(first, jnp.zeros_like(prev_blk), prev_blk)

    def window(prev_blk, cur_ref, idx, nb):
        if nb > 0:
            return cur_ref[idx + (slice((nb - 1) * QBLK, (nb + 1) * QBLK), slice(None))]
        return jnp.concatenate([before_sequence(prev_blk),
                                cur_ref[idx + (slice(0, QBLK), slice(None))]], axis=0)

    def window1(prev_ref, cur_ref, nb):
        pieces = []
        for r in range(4):
            if nb > 0:
                pieces.append(cur_ref[0, 0, r, (nb - 1) * PIECE:(nb + 1) * PIECE, :])
            else:
                pieces += [before_sequence(prev_ref[0, 0, r, QBLK - PIECE:QBLK, :]),
                           cur_ref[0, 0, r, 0:PIECE, :]]
        return jnp.concatenate(pieces, axis=0)

    def block(b, q, kw, vw, nb):
        zero = jnp.zeros_like(q)
        qq = jnp.concatenate([jnp.where(head0, q, zero), jnp.where(head0, zero, q)], axis=0)
        s = lax.dot_general(qq, kw, (((1,), (1,)), ((), ())), preferred_element_type=F32)
        s = s + bias_ref[b, not_first if nb == 0 else 1]
        m = jnp.max(s, axis=1, keepdims=True)
        p = jnp.exp2(s - m).astype(BF16)
        acc = jnp.dot(p, jnp.concatenate([vw, ones], axis=1), preferred_element_type=F32)
        o_t = jnp.where(head0, acc[0:QBLK, 0:LANES], acc[QBLK:WIN, 0:LANES])
        l_t = jnp.where(head0, acc[0:QBLK, LANES:], acc[QBLK:WIN, LANES:])
        m_t = jnp.where(head0, m[0:QBLK], m[QBLK:WIN])
        return o_t / l_t, m_t + jnp.log2(l_t)

    for r in range(16):
        o_t, lse_t = block(2, qb[0, 0, r], window(kbp[0, 0, r], kb, (0, 0, r), 0),
                           window(vbp[0, 0, r], vb, (0, 0, r), 0), 0)
        rows = pl.ds((r % 4) * (SPAN // 4) + r // 4, QBLK, stride=4)
        ob[1, rows, :] = o_t
        lb[1, rows, :] = lse_t
    for r in range(4):
        for nb in range(A_BLOCKS):
            q = qa[0, 0, r, nb * QBLK:(nb + 1) * QBLK, :]
            o_t, lse_t = block(1, q, window(kap[0, 0, r], ka, (0, 0, r), nb),
                               window(vap[0, 0, r], va, (0, 0, r), nb), nb)
            rows = slice((r * A_BLOCKS + nb) * QBLK, (r * A_BLOCKS + nb + 1) * QBLK)
            ob[0, rows, :] = o_t
            lb[0, rows, :] = lse_t
    for nb in range(SPAN // QBLK):
        q = jnp.concatenate([qa[0, 0, r, nb * PIECE:(nb + 1) * PIECE, :] for r in range(4)], axis=0)
        o_t, lse_t = block(0, q, window1(kap, ka, nb), window1(vap, va, nb), nb)
        for r in range(4):
            rows = slice(r * (SPAN // 4) + nb * PIECE, r * (SPAN // 4) + (nb + 1) * PIECE)
            l0, l1, l2 = lse_t[r * PIECE:(r + 1) * PIECE], lb[0, rows, :], lb[1, rows, :]
            mx = jnp.maximum(jnp.maximum(l0, l1), l2)
            w0, w1, w2 = jnp.exp2(l0 - mx), jnp.exp2(l1 - mx), jnp.exp2(l2 - mx)
            num = w0 * o_t[r * PIECE:(r + 1) * PIECE] + w1 * ob[0, rows, :] + w2 * ob[1, rows, :]
            tokens = pl.ds(r + 4 * nb * PIECE, PIECE, stride=4)
            o_ref[0, tokens, :] = num / (w0 + w1 + w2)


def _attention(qkv_layouts, slopes, batch, seq):
    qa, ka, va, qb, kb, vb = qkv_layouts
    n_slabs = qa.shape[0]
    n = batch * seq
    assert seq % SPAN == 0
    spans = seq // SPAN

    def cur(block_shape):
        zeros = (0,) * (len(block_shape) - 2)
        return pl.BlockSpec(block_shape, lambda b, s, t: (s, b * spans + t) + zeros)

    a_spec = cur((1, 1, 4, SPAN // 4, LANES))
    b_spec = cur((1, 1, 16, QBLK, LANES))
    ap_spec = pl.BlockSpec((1, 1, 4, QBLK, LANES),
                           lambda b, s, t: (s, jnp.maximum(b * spans + t - 1, 0), 0, A_BLOCKS - 1, 0))
    bp_spec = pl.BlockSpec((1, 1, 16, QBLK, LANES),
                           lambda b, s, t: (s, jnp.maximum(b * spans + t - 1, 0), 0, 0, 0))
    return pl.pallas_call(
        _attn_kernel,
        grid=(batch, n_slabs, spans),
        in_specs=[pl.BlockSpec(memory_space=pltpu.SMEM),
                  a_spec, a_spec, a_spec, ap_spec, ap_spec,
                  b_spec, b_spec, b_spec, bp_spec, bp_spec],
        out_specs=pl.BlockSpec((1, SPAN, LANES), lambda b, s, t: (s, b * spans + t, 0)),
        out_shape=jax.ShapeDtypeStruct((n_slabs, n, LANES), F32),
        scratch_shapes=[
            pltpu.VMEM((len(DILATIONS), 2, WIN, WIN), F32),
            pltpu.VMEM((len(DILATIONS) - 1, SPAN, LANES), F32),
            pltpu.VMEM((len(DILATIONS) - 1, SPAN, LANES), F32),
        ],
        compiler_params=pltpu.CompilerParams(
            dimension_semantics=("arbitrary", "arbitrary", "arbitrary"),
            vmem_limit_bytes=VMEM_LIMIT),
        name="attn",
    )(slopes, qa, ka, va, ka, va, qb, kb, vb, kb, vb)


def _tail_kernel(x_ref, attn_ref, sgu_ref, p_ref, ga_ref, wout_ref, gpm_ref, gpf_ref,
                 wgu_ref, wd_ref, gpo_ref, wpg_ref, bpg_ref, wpp_ref, o_ref):
    d_ff = wd_ref.shape[0]
    bounds = (0,) + TAIL_SPLITS + (x_ref.shape[0],)
    chains = [slice(lo, hi) for lo, hi in zip(bounds[:-1], bounds[1:])]

    def dot(a, w_ref):
        return jnp.dot(a, w_ref[...], preferred_element_type=F32)

    pe = [dot(p_ref[rows, :].astype(BF16), wpp_ref) for rows in chains]
    groups = []
    for rows in chains:
        attn = jnp.concatenate([attn_ref[s, rows, :] for s in range(attn_ref.shape[0])], axis=-1)
        an = _rms(attn, ga_ref[...]).astype(BF16)
        groups.append(jnp.concatenate([an, sgu_ref[rows, :]], axis=-1))
    mixed = [dot(g, wout_ref) for g in groups]
    h = [x_ref[rows, :] + _rms(m, gpm_ref[...]) for rows, m in zip(chains, mixed)]
    f = [_rms(hc, gpf_ref[...]).astype(BF16) for hc in h]
    gu = [dot(fc, wgu_ref) for fc in f]
    hid = [(jax.nn.silu(g[:, :d_ff]) * g[:, d_ff:]).astype(BF16) for g in gu]
    y = [dot(hc, wd_ref) for hc in hid]
    h = [hc + _rms(yc, gpo_ref[...]) for hc, yc in zip(h, y)]
    gate = [jax.nn.sigmoid(dot(hc.astype(BF16), wpg_ref) + bpg_ref[...]) for hc in h]
    for rows, hc, gc, pc in zip(chains, h, gate, pe):
        o_ref[rows, :] = hc + gc * pc


def _tail(x2, attn, sgu, p2, *params):
    n, d = x2.shape
    tm = TAIL_TILE

    def tile(a):
        return pl.BlockSpec((tm, a.shape[1]), lambda i: (i, 0))

    return pl.pallas_call(
        _tail_kernel,
        grid=(n // tm,),
        in_specs=[tile(x2), pl.BlockSpec((attn.shape[0], tm, LANES), lambda i: (0, i, 0)),
                  tile(sgu), tile(p2)] + [_const_spec(a.shape) for a in params],
        out_specs=pl.BlockSpec((tm, d), lambda i: (i, 0)),
        out_shape=jax.ShapeDtypeStruct((n, d), F32),
        compiler_params=pltpu.CompilerParams(dimension_semantics=("arbitrary",),
                                             vmem_limit_bytes=VMEM_LIMIT),
        name="tail",
    )(x2, attn, sgu, p2, *params)


def _group_major(w_in, attn_w, sgu_w):
    gd = sgu_w // N_SGU_GROUPS
    u0 = 3 * attn_w
    z0 = u0 + sgu_w
    cols = []
    for g in range(N_SGU_GROUPS):
        cols += [w_in[:, u0 + g * gd:u0 + (g + 1) * gd], w_in[:, z0 + g * gd:z0 + (g + 1) * gd]]
    return jnp.concatenate(cols + [w_in[:, :u0]], axis=1)


def kernel(x, p, ln_pre_mix, w_in, sgu_ln_g, sgu_ln_b, w_spatial, b_spatial, attn_out_norm,
           sgu_out_norm, w_out, ln_post_mix, ln_pre_ffn, w_gate_up, w_down, ln_post_ffn,
           w_pe_gate, b_pe_gate, w_pe_proj):
    batch, seq, d = x.shape
    depth = w_in.shape[0]
    attn_w = attn_out_norm.shape[1]
    sgu_w = sgu_out_norm.shape[1]
    n_heads = attn_w // HEAD_DIM
    slopes = 2.0 ** (-8.0 * (jnp.arange(n_heads, dtype=F32) + 1.0) / n_heads)

    def row(a):
        return a.reshape(1, -1)

    h = x.reshape(batch * seq, d)
    for i in range(depth):
        tail_weights = (w_out[i], w_gate_up[i], w_down[i], w_pe_gate[i], w_pe_proj[i])
        outs = _proj_sgu(
            h, row(ln_pre_mix[i]), _group_major(w_in[i], attn_w, sgu_w).astype(BF16),
            row(sgu_ln_g[i]), row(sgu_ln_b[i]),
            w_spatial[i], b_spatial[i][:, :, None], row(sgu_out_norm[i]), attn_w, sgu_w,
            tail_weights)
        qkv_layouts, sgu = outs[:6], outs[6]
        wout_b, wgu_b, wd_b, wpg_b, wpp_b = outs[7:]
        attn = _attention(qkv_layouts, slopes, batch, seq)
        h = _tail(h, attn, sgu, p[i].reshape(batch * seq, -1),
                  row(attn_out_norm[i]), wout_b, row(ln_post_mix[i]),
                  row(ln_pre_ffn[i]), wgu_b, wd_b,
                  row(ln_post_ffn[i]), wpg_b, row(b_pe_gate[i]), wpp_b)
    return h.reshape(batch, seq, d)
```

```python
import functools

import jax
import jax.numpy as jnp
from jax import lax
from jax.experimental import pallas as pl
from jax.experimental.pallas import tpu as pltpu

F32 = jnp.float32
BF16 = jnp.bfloat16

HEAD_DIM = 64
QBLK = 128
WIN = 2 * QBLK
DILATIONS = (1, 4, 16)
SPAN = QBLK * DILATIONS[-1]
SGU_CHUNK = 128
N_SGU_GROUPS = 4
EPS = 1e-6
NEG = -1e30
LOG2E = 1.4426950408889634
Q_SCALE = HEAD_DIM ** -0.5 * LOG2E

LANES = 128
BF16_SUBLANES = 16
HEADS_PER_SLAB = LANES // HEAD_DIM

PROJ_TILE = 512
TAIL_TILE = 512
TAIL_SPLITS = (256,)
VMEM_LIMIT = 56 * 1024 * 1024


def _rms(x, g):
    return x * lax.rsqrt(jnp.mean(x * x, axis=-1, keepdims=True) + EPS) * g


def _const_spec(shape):
    zeros = (0,) * len(shape)
    return pl.BlockSpec(shape, lambda *_: zeros, pipeline_mode=pl.Buffered(1))


def _proj_sgu_kernel(*refs, attn_w, sgu_w, n_cast):
    x_ref, g_ref, w_ref, lng_ref, lnb_ref, ws_ref, bs_ref, gout_ref = refs[:8]
    cast_in = refs[8:8 + n_cast]
    qa, ka, va, qb, kb, vb, sgu_ref = refs[8 + n_cast:15 + n_cast]
    cast_out = refs[15 + n_cast:15 + 2 * n_cast]
    tok_tmp, res_tmp = refs[15 + 2 * n_cast:]
    for src, dst in zip(cast_in, cast_out):
        dst[...] = src[...].astype(dst.dtype)
    tm = x_ref.shape[0]
    gd = sgu_w // N_SGU_GROUPS
    a = _rms(x_ref[...], g_ref[...]).astype(BF16)

    def project(c0, width):
        return jnp.dot(a, w_ref[:, c0:c0 + width], preferred_element_type=F32)

    def emit_layouts(idx, a_ref, b_ref, cols):
        for s in range(attn_w // LANES):
            tok_tmp[idx, s] = cols[:, s * LANES:(s + 1) * LANES]
            for r in range(4):
                every4 = tok_tmp[idx, s, pl.ds(r, tm // 4, stride=4), :]
                a_ref[s, 0, r] = every4.astype(BF16)
                res_tmp[idx, s, r] = every4
                for r2 in range(4):
                    b_ref[s, 0, 4 * r2 + r] = (
                        res_tmp[idx, s, r, pl.ds(r2, tm // 16, stride=4), :].astype(BF16))

    row = lax.broadcasted_iota(jnp.int32, (SGU_CHUNK, SGU_CHUNK), 0)
    col = lax.broadcasted_iota(jnp.int32, (SGU_CHUNK, SGU_CHUNK), 1)
    causal = row >= col
    def gate(g, uz):
        u = jax.nn.gelu(uz[:, :gd])
        z = jax.nn.gelu(uz[:, gd:])
        zc = z - jnp.mean(z, axis=-1, keepdims=True)
        zn = zc * lax.rsqrt(jnp.mean(zc * zc, axis=-1, keepdims=True) + EPS)
        zn = (zn * lng_ref[...] + lnb_ref[...]).astype(BF16)
        wm = jnp.where(causal, ws_ref[g], 0.0).astype(BF16)
        chunks = [zn[c * SGU_CHUNK:(c + 1) * SGU_CHUNK] for c in range(tm // SGU_CHUNK)]
        mixed = jnp.dot(wm, jnp.concatenate(chunks, axis=1), preferred_element_type=F32)
        mixed = [mixed[:, c * gd:(c + 1) * gd] + bs_ref[g] for c in range(len(chunks))]
        return u * jnp.concatenate(mixed, axis=0)

    uz01 = project(0, 4 * gd)
    uz23 = project(4 * gd, 4 * gd)
    outs = [gate(0, uz01[:, :2 * gd])]
    emit_layouts(0, qa, qb, project(2 * sgu_w, attn_w) * Q_SCALE)
    outs.append(gate(1, uz01[:, 2 * gd:]))
    emit_layouts(1, ka, kb, project(2 * sgu_w + attn_w, attn_w))
    outs.append(gate(2, uz23[:, :2 * gd]))
    outs.append(gate(3, uz23[:, 2 * gd:]))
    ssq = sum(jnp.sum(o * o, axis=-1, keepdims=True) for o in outs)
    scale = lax.rsqrt(ssq / sgu_w + EPS)
    for g in range(N_SGU_GROUPS):
        sgu_ref[:, g * gd:(g + 1) * gd] = (outs[g] * scale * gout_ref[:, g * gd:(g + 1) * gd]).astype(BF16)
    emit_layouts(2, va, vb, project(2 * sgu_w + 2 * attn_w, attn_w))


def _cast_chunk_spec(w, steps):
    rows = w.shape[0]
    chunk = max(BF16_SUBLANES, -(-rows // steps // BF16_SUBLANES) * BF16_SUBLANES)
    while rows % chunk:
        chunk += BF16_SUBLANES
    last = rows // chunk - 1
    return pl.BlockSpec((chunk, w.shape[1]), lambda i: (jnp.minimum(i, last), 0))


def _proj_sgu(x2, g_pre, w_in, ln_g, ln_b, w_sp, b_sp, g_sgu, attn_w, sgu_w, to_cast):
    n, d = x2.shape
    tm = PROJ_TILE
    per_span = SPAN // tm
    cast_specs = [_cast_chunk_spec(w, n // tm) for w in to_cast]
    n_slabs = attn_w // LANES
    a_shape = jax.ShapeDtypeStruct((n_slabs, n // SPAN, 4, SPAN // 4, LANES), BF16)
    b_shape = jax.ShapeDtypeStruct((n_slabs, n // SPAN, 16, SPAN // 16, LANES), BF16)
    a_spec = pl.BlockSpec((n_slabs, 1, 4, tm // 4, LANES),
                          lambda i: (0, i // per_span, 0, i % per_span, 0))
    b_spec = pl.BlockSpec((n_slabs, 1, 16, tm // 16, LANES),
                          lambda i: (0, i // per_span, 0, i % per_span, 0))
    return pl.pallas_call(
        functools.partial(_proj_sgu_kernel, attn_w=attn_w, sgu_w=sgu_w, n_cast=len(to_cast)),
        grid=(n // tm,),
        in_specs=[
            pl.BlockSpec((tm, d), lambda i: (i, 0)),
            _const_spec(g_pre.shape), _const_spec(w_in.shape), _const_spec(ln_g.shape),
            _const_spec(ln_b.shape), _const_spec(w_sp.shape), _const_spec(b_sp.shape),
            _const_spec(g_sgu.shape),
        ] + cast_specs,
        out_specs=[a_spec] * 3 + [b_spec] * 3
        + [pl.BlockSpec((tm, sgu_w), lambda i: (i, 0))] + cast_specs,
        out_shape=[a_shape] * 3 + [b_shape] * 3
        + [jax.ShapeDtypeStruct((n, sgu_w), BF16)]
        + [jax.ShapeDtypeStruct(w.shape, BF16) for w in to_cast],
        scratch_shapes=[pltpu.VMEM((3, n_slabs, tm, LANES), F32),
                        pltpu.VMEM((3, n_slabs, 4, tm // 4, LANES), F32)],
        compiler_params=pltpu.CompilerParams(dimension_semantics=("arbitrary",),
                                             vmem_limit_bytes=VMEM_LIMIT),
        name="proj_sgu",
    )(x2, g_pre, w_in, ln_g, ln_b, w_sp, b_sp, g_sgu, *to_cast)


A_BLOCKS = SPAN // 4 // QBLK
PIECE = QBLK // 4


def _attn_kernel(slopes_ref, qa, ka, va, kap, vap, qb, kb, vb, kbp, vbp,
                 o_ref, bias_ref, ob, lb):
    slab = pl.program_id(1)
    span = pl.program_id(2)
    lane = lax.broadcasted_iota(jnp.int32, (QBLK, LANES), 1)
    head0 = lane < HEAD_DIM
    first = span == 0

    @pl.when(first)
    def _start_of_sequence():
        i = lax.broadcasted_iota(jnp.int32, (WIN, WIN), 0)
        j = lax.broadcasted_iota(jnp.int32, (WIN, WIN), 1)
        slope = jnp.where(i < QBLK, slopes_ref[HEADS_PER_SLAB * slab],
                          slopes_ref[HEADS_PER_SLAB * slab + 1])
        q_pos = i & (QBLK - 1)
        for b, d in enumerate(DILATIONS):
            if d == 1:
                q_at = 4 * (q_pos & (PIECE - 1)) + q_pos // PIECE
                k_at = 4 * (j & (2 * PIECE - 1)) + j // (2 * PIECE)
            else:
                q_at, k_at = q_pos, j
            steps = QBLK + q_at - k_at
            valid = (steps >= 0) & (steps <= QBLK)
            bias = jnp.where(valid, -(slope * LOG2E) * (steps * d).astype(F32), NEG)
            bias_ref[b, 1] = bias
            bias_ref[b, 0] = jnp.where(k_at >= QBLK, bias, NEG)

    not_first = jnp.where(first, 0, 1)
    ones = jnp.ones((WIN, LANES), BF16)

    def before_sequence(prev_blk):
        return jnp.where(first, jnp.zeros_like(prev_blk), prev_blk)

    def window(prev_blk, cur_ref, idx, nb):
        if nb > 0:
            return cur_ref[idx + (slice((nb - 1) * QBLK, (nb + 1) * QBLK), slice(None))]
        return jnp.concatenate([before_sequence(prev_blk),
                                cur_ref[idx + (slice(0, QBLK), slice(None))]], axis=0)

    def window1(prev_ref, cur_ref, nb):
        pieces = []
        for r in range(4):
            if nb > 0:
                pieces.append(cur_ref[0, 0, r, (nb - 1) * PIECE:(nb + 1) * PIECE, :])
            else:
                pieces += [before_sequence(prev_ref[0, 0, r, QBLK - PIECE:QBLK, :]),
                           cur_ref[0, 0, r, 0:PIECE, :]]
        return jnp.concatenate(pieces, axis=0)

    def block(b, q, kw, vw, nb):
        zero = jnp.zeros_like(q)
        qq = jnp.concatenate([jnp.where(head0, q, zero), jnp.where(head0, zero, q)], axis=0)
        s = lax.dot_general(qq, kw, (((1,), (1,)), ((), ())), preferred_element_type=F32)
        s = s + bias_ref[b, not_first if nb == 0 else 1]
        m = jnp.max(s, axis=1, keepdims=True)
        p = jnp.exp2(s - m).astype(BF16)
        acc = jnp.dot(p, jnp.concatenate([vw, ones], axis=1), preferred_element_type=F32)
        o_t = jnp.where(head0, acc[0:QBLK, 0:LANES], acc[QBLK:WIN, 0:LANES])
        l_t = jnp.where(head0, acc[0:QBLK, LANES:], acc[QBLK:WIN, LANES:])
        m_t = jnp.where(head0, m[0:QBLK], m[QBLK:WIN])
        return o_t, l_t, m_t

    def normalised(o_t, l_t, m_t):
        return o_t / l_t, m_t + jnp.log2(l_t)

    for r in range(16):
        o_t, lse_t = normalised(*block(2, qb[0, 0, r], window(kbp[0, 0, r], kb, (0, 0, r), 0),
                                       window(vbp[0, 0, r], vb, (0, 0, r), 0), 0))
        rows = pl.ds((r % 4) * (SPAN // 4) + r // 4, QBLK, stride=4)
        ob[1, rows, :] = o_t
        lb[1, rows, :] = lse_t
    for r in range(4):
        for nb in range(A_BLOCKS):
            q = qa[0, 0, r, nb * QBLK:(nb + 1) * QBLK, :]
            o_t, lse_t = normalised(*block(1, q, window(kap[0, 0, r], ka, (0, 0, r), nb),
                                           window(vap[0, 0, r], va, (0, 0, r), nb), nb))
            rows = slice((r * A_BLOCKS + nb) * QBLK, (r * A_BLOCKS + nb + 1) * QBLK)
            ob[0, rows, :] = o_t
            lb[0, rows, :] = lse_t
    for nb in range(SPAN // QBLK):
        q = jnp.concatenate([qa[0, 0, r, nb * PIECE:(nb + 1) * PIECE, :] for r in range(4)], axis=0)
        o_t, l_t, m_t = block(0, q, window1(kap, ka, nb), window1(vap, va, nb), nb)
        for r in range(4):
            rows = slice(r * (SPAN // 4) + nb * PIECE, r * (SPAN // 4) + (nb + 1) * PIECE)
            piece = slice(r * PIECE, (r + 1) * PIECE)
            m0, l1, l2 = m_t[piece], lb[0, rows, :], lb[1, rows, :]
            mx = jnp.maximum(jnp.maximum(m0, l1), l2)
            w0, w1, w2 = jnp.exp2(m0 - mx), jnp.exp2(l1 - mx), jnp.exp2(l2 - mx)
            num = w0 * o_t[piece] + w1 * ob[0, rows, :] + w2 * ob[1, rows, :]
            tokens = pl.ds(r + 4 * nb * PIECE, PIECE, stride=4)
            o_ref[0, tokens, :] = num / (w0 * l_t[piece] + w1 + w2)


def _attention(qkv_layouts, slopes, batch, seq):
    qa, ka, va, qb, kb, vb = qkv_layouts
    n_slabs = qa.shape[0]
    n = batch * seq
    assert seq % SPAN == 0
    spans = seq // SPAN

    def cur(block_shape):
        zeros = (0,) * (len(block_shape) - 2)
        return pl.BlockSpec(block_shape, lambda b, s, t: (s, b * spans + t) + zeros)

    a_spec = cur((1, 1, 4, SPAN // 4, LANES))
    b_spec = cur((1, 1, 16, QBLK, LANES))
    ap_spec = pl.BlockSpec((1, 1, 4, QBLK, LANES),
                           lambda b, s, t: (s, jnp.maximum(b * spans + t - 1, 0), 0, A_BLOCKS - 1, 0))
    bp_spec = pl.BlockSpec((1, 1, 16, QBLK, LANES),
                           lambda b, s, t: (s, jnp.maximum(b * spans + t - 1, 0), 0, 0, 0))
    return pl.pallas_call(
        _attn_kernel,
        grid=(batch, n_slabs, spans),
        in_specs=[pl.BlockSpec(memory_space=pltpu.SMEM),
                  a_spec, a_spec, a_spec, ap_spec, ap_spec,
                  b_spec, b_spec, b_spec, bp_spec, bp_spec],
        out_specs=pl.BlockSpec((1, SPAN, LANES), lambda b, s, t: (s, b * spans + t, 0)),
        out_shape=jax.ShapeDtypeStruct((n_slabs, n, LANES), F32),
        scratch_shapes=[
            pltpu.VMEM((len(DILATIONS), 2, WIN, WIN), F32),
            pltpu.VMEM((len(DILATIONS) - 1, SPAN, LANES), F32),
            pltpu.VMEM((len(DILATIONS) - 1, SPAN, LANES), F32),
        ],
        compiler_params=pltpu.CompilerParams(
            dimension_semantics=("arbitrary", "arbitrary", "arbitrary"),
            vmem_limit_bytes=VMEM_LIMIT),
        name="attn",
    )(slopes, qa, ka, va, ka, va, qb, kb, vb, kb, vb)


def _tail_kernel(x_ref, attn_ref, sgu_ref, p_ref, ga_ref, wout_ref, gpm_ref, gpf_ref,
                 wgu_ref, wd_ref, gpo_ref, wpg_ref, bpg_ref, wpp_ref, o_ref):
    d_ff = wd_ref.shape[0]
    bounds = (0,) + TAIL_SPLITS + (x_ref.shape[0],)
    chains = [slice(lo, hi) for lo, hi in zip(bounds[:-1], bounds[1:])]

    def dot(a, w_ref):
        return jnp.dot(a, w_ref[...], preferred_element_type=F32)

    pe = [dot(p_ref[rows, :].astype(BF16), wpp_ref) for rows in chains]
    groups = []
    for rows in chains:
        attn = jnp.concatenate([attn_ref[s, rows, :] for s in range(attn_ref.shape[0])], axis=-1)
        an = _rms(attn, ga_ref[...]).astype(BF16)
        groups.append(jnp.concatenate([an, sgu_ref[rows, :]], axis=-1))
    mixed = [dot(g, wout_ref) for g in groups]
    h = [x_ref[rows, :] + _rms(m, gpm_ref[...]) for rows, m in zip(chains, mixed)]
    f = [_rms(hc, gpf_ref[...]).astype(BF16) for hc in h]
    gu = [dot(fc, wgu_ref) for fc in f]
    hid = [(jax.nn.silu(g[:, :d_ff]) * g[:, d_ff:]).astype(BF16) for g in gu]
    y = [dot(hc, wd_ref) for hc in hid]
    h = [hc + _rms(yc, gpo_ref[...]) for hc, yc in zip(h, y)]
    gate = [jax.nn.sigmoid(dot(hc.astype(BF16), wpg_ref) + bpg_ref[...]) for hc in h]
    for rows, hc, gc, pc in zip(chains, h, gate, pe):
        o_ref[rows, :] = hc + gc * pc


def _tail(x2, attn, sgu, p2, *params):
    n, d = x2.shape
    tm = TAIL_TILE

    def tile(a):
        return pl.BlockSpec((tm, a.shape[1]), lambda i: (i, 0))

    return pl.pallas_call(
        _tail_kernel,
        grid=(n // tm,),
        in_specs=[tile(x2), pl.BlockSpec((attn.shape[0], tm, LANES), lambda i: (0, i, 0)),
                  tile(sgu), tile(p2)] + [_const_spec(a.shape) for a in params],
        out_specs=pl.BlockSpec((tm, d), lambda i: (i, 0)),
        out_shape=jax.ShapeDtypeStruct((n, d), F32),
        compiler_params=pltpu.CompilerParams(dimension_semantics=("arbitrary",),
                                             vmem_limit_bytes=VMEM_LIMIT),
        name="tail",
    )(x2, attn, sgu, p2, *params)


def _group_major(w_in, attn_w, sgu_w):
    gd = sgu_w // N_SGU_GROUPS
    u0 = 3 * attn_w
    z0 = u0 + sgu_w
    cols = []
    for g in range(N_SGU_GROUPS):
        cols += [w_in[:, u0 + g * gd:u0 + (g + 1) * gd], w_in[:, z0 + g * gd:z0 + (g + 1) * gd]]
    return jnp.concatenate(cols + [w_in[:, :u0]], axis=1)


def kernel(x, p, ln_pre_mix, w_in, sgu_ln_g, sgu_ln_b, w_spatial, b_spatial, attn_out_norm,
           sgu_out_norm, w_out, ln_post_mix, ln_pre_ffn, w_gate_up, w_down, ln_post_ffn,
           w_pe_gate, b_pe_gate, w_pe_proj):
    batch, seq, d = x.shape
    depth = w_in.shape[0]
    attn_w = attn_out_norm.shape[1]
    sgu_w = sgu_out_norm.shape[1]
    n_heads = attn_w // HEAD_DIM
    slopes = 2.0 ** (-8.0 * (jnp.arange(n_heads, dtype=F32) + 1.0) / n_heads)

    def row(a):
        return a.reshape(1, -1)

    h = x.reshape(batch * seq, d)
    for i in range(depth):
        tail_weights = (w_out[i], w_gate_up[i], w_down[i], w_pe_gate[i], w_pe_proj[i])
        outs = _proj_sgu(
            h, row(ln_pre_mix[i]), _group_major(w_in[i], attn_w, sgu_w).astype(BF16),
            row(sgu_ln_g[i]), row(sgu_ln_b[i]),
            w_spatial[i], b_spatial[i][:, :, None], row(sgu_out_norm[i]), attn_w, sgu_w,
            tail_weights)
        qkv_layouts, sgu = outs[:6], outs[6]
        wout_b, wgu_b, wd_b, wpg_b, wpp_b = outs[7:]
        attn = _attention(qkv_layouts, slopes, batch, seq)
        h = _tail(h, attn, sgu, p[i].reshape(batch * seq, -1),
                  row(attn_out_norm[i]), wout_b, row(ln_post_mix[i]),
                  row(ln_pre_ffn[i]), wgu_b, wd_b,
                  row(ln_post_ffn[i]), wpg_b, row(b_pe_gate[i]), wpp_b)
    return h.reshape(batch, seq, d)
```

```python
import functools

import jax
import jax.numpy as jnp
from jax import lax
from jax.experimental import pallas as pl
from jax.experimental.pallas import tpu as pltpu

F32 = jnp.float32
BF16 = jnp.bfloat16

HEAD_DIM = 64
QBLK = 128
WIN = 2 * QBLK
DILATIONS = (1, 4, 16)
SPAN = QBLK * DILATIONS[-1]
SGU_CHUNK = 128
N_SGU_GROUPS = 4
EPS = 1e-6
NEG = -1e30
LOG2E = 1.4426950408889634
Q_SCALE = HEAD_DIM ** -0.5 * LOG2E

LANES = 128
BF16_SUBLANES = 16
HEADS_PER_SLAB = LANES // HEAD_DIM

PROJ_TILE = 512
TAIL_TILE = 512
TAIL_SPLITS = (256,)
VMEM_LIMIT = 56 * 1024 * 1024


def _rms(x, g):
    return x * lax.rsqrt(jnp.mean(x * x, axis=-1, keepdims=True) + EPS) * g


def _const_spec(shape):
    zeros = (0,) * len(shape)
    return pl.BlockSpec(shape, lambda *_: zeros, pipeline_mode=pl.Buffered(1))


def _proj_sgu_kernel(*refs, attn_w, sgu_w, n_cast):
    x_ref, g_ref, w_ref, lng_ref, lnb_ref, ws_ref, bs_ref, gout_ref = refs[:8]
    cast_in = refs[8:8 + n_cast]
    qa, ka, va, qb, kb, vb, sgu_ref = refs[8 + n_cast:15 + n_cast]
    cast_out = refs[15 + n_cast:15 + 2 * n_cast]
    tok_tmp, res_tmp = refs[15 + 2 * n_cast:]
    for src, dst in zip(cast_in, cast_out):
        dst[...] = src[...].astype(dst.dtype)
    tm = x_ref.shape[0]
    gd = sgu_w // N_SGU_GROUPS
    a = _rms(x_ref[...], g_ref[...]).astype(BF16)

    def project(c0, width):
        return jnp.dot(a, w_ref[:, c0:c0 + width], preferred_element_type=F32)

    def emit_layouts(idx, a_ref, b_ref, cols):
        for s in range(attn_w // LANES):
            tok_tmp[idx, s] = cols[:, s * LANES:(s + 1) * LANES]
            for r in range(4):
                every4 = tok_tmp[idx, s, pl.ds(r, tm // 4, stride=4), :]
                a_ref[s, 0, r] = every4.astype(BF16)
                res_tmp[idx, s, r] = every4
                for r2 in range(4):
                    b_ref[s, 0, 4 * r2 + r] = (
                        res_tmp[idx, s, r, pl.ds(r2, tm // 16, stride=4), :].astype(BF16))

    row = lax.broadcasted_iota(jnp.int32, (SGU_CHUNK, SGU_CHUNK), 0)
    col = lax.broadcasted_iota(jnp.int32, (SGU_CHUNK, SGU_CHUNK), 1)
    causal = row >= col
    def gate(g, uz):
        u = jax.nn.gelu(uz[:, :gd])
        z = jax.nn.gelu(uz[:, gd:])
        zc = z - jnp.mean(z, axis=-1, keepdims=True)
        zn = zc * lax.rsqrt(jnp.mean(zc * zc, axis=-1, keepdims=True) + EPS)
        zn = (zn * lng_ref[...] + lnb_ref[...]).astype(BF16)
        wm = jnp.where(causal, ws_ref[g], 0.0).astype(BF16)
        chunks = [zn[c * SGU_CHUNK:(c + 1) * SGU_CHUNK] for c in range(tm // SGU_CHUNK)]
        mixed = jnp.dot(wm, jnp.concatenate(chunks, axis=1), preferred_element_type=F32)
        mixed = [mixed[:, c * gd:(c + 1) * gd] + bs_ref[g] for c in range(len(chunks))]
        return u * jnp.concatenate(mixed, axis=0)

    uz01 = project(0, 4 * gd)
    uz23 = project(4 * gd, 4 * gd)
    outs = [gate(0, uz01[:, :2 * gd])]
    emit_layouts(0, qa, qb, project(2 * sgu_w, attn_w) * Q_SCALE)
    outs.append(gate(1, uz01[:, 2 * gd:]))
    emit_layouts(1, ka, kb, project(2 * sgu_w + attn_w, attn_w))
    outs.append(gate(2, uz23[:, :2 * gd]))
    outs.append(gate(3, uz23[:, 2 * gd:]))
    ssq = sum(jnp.sum(o * o, axis=-1, keepdims=True) for o in outs)
    scale = lax.rsqrt(ssq / sgu_w + EPS)
    for g in range(N_SGU_GROUPS):
        sgu_ref[:, g * gd:(g + 1) * gd] = (outs[g] * scale * gout_ref[:, g * gd:(g + 1) * gd]).astype(BF16)
    emit_layouts(2, va, vb, project(2 * sgu_w + 2 * attn_w, attn_w))


def _cast_chunk_spec(w, steps):
    rows = w.shape[0]
    chunk = max(BF16_SUBLANES, -(-rows // steps // BF16_SUBLANES) * BF16_SUBLANES)
    while rows % chunk:
        chunk += BF16_SUBLANES
    last = rows // chunk - 1
    return pl.BlockSpec((chunk, w.shape[1]), lambda i: (jnp.minimum(i, last), 0))


def _proj_sgu(x2, g_pre, w_in, ln_g, ln_b, w_sp, b_sp, g_sgu, attn_w, sgu_w, to_cast):
    n, d = x2.shape
    tm = PROJ_TILE
    per_span = SPAN // tm
    cast_specs = [_cast_chunk_spec(w, n // tm) for w in to_cast]
    n_slabs = attn_w // LANES
    a_shape = jax.ShapeDtypeStruct((n_slabs, n // SPAN, 4, SPAN // 4, LANES), BF16)
    b_shape = jax.ShapeDtypeStruct((n_slabs, n // SPAN, 16, SPAN // 16, LANES), BF16)
    a_spec = pl.BlockSpec((n_slabs, 1, 4, tm // 4, LANES),
                          lambda i: (0, i // per_span, 0, i % per_span, 0))
    b_spec = pl.BlockSpec((n_slabs, 1, 16, tm // 16, LANES),
                          lambda i: (0, i // per_span, 0, i % per_span, 0))
    return pl.pallas_call(
        functools.partial(_proj_sgu_kernel, attn_w=attn_w, sgu_w=sgu_w, n_cast=len(to_cast)),
        grid=(n // tm,),
        in_specs=[
            pl.BlockSpec((tm, d), lambda i: (i, 0)),
            _const_spec(g_pre.shape), _const_spec(w_in.shape), _const_spec(ln_g.shape),
            _const_spec(ln_b.shape), _const_spec(w_sp.shape), _const_spec(b_sp.shape),
            _const_spec(g_sgu.shape),
        ] + cast_specs,
        out_specs=[a_spec] * 3 + [b_spec] * 3
        + [pl.BlockSpec((tm, sgu_w), lambda i: (i, 0))] + cast_specs,
        out_shape=[a_shape] * 3 + [b_shape] * 3
        + [jax.ShapeDtypeStruct((n, sgu_w), BF16)]
        + [jax.ShapeDtypeStruct(w.shape, BF16) for w in to_cast],
        scratch_shapes=[pltpu.VMEM((3, n_slabs, tm, LANES), F32),
                        pltpu.VMEM((3, n_slabs, 4, tm // 4, LANES), F32)],
        compiler_params=pltpu.CompilerParams(
            dimension_semantics=("arbitrary",), vmem_limit_bytes=VMEM_LIMIT,
            allow_input_fusion=[k == 2 for k in range(8 + len(to_cast))]),
        name="proj_sgu",
    )(x2, g_pre, w_in, ln_g, ln_b, w_sp, b_sp, g_sgu, *to_cast)


A_BLOCKS = SPAN // 4 // QBLK
PIECE = QBLK // 4


def _attn_kernel(slopes_ref, qa, ka, va, kap, vap, qb, kb, vb, kbp, vbp,
                 o_ref, bias_ref, ob, lb):
    slab = pl.program_id(1)
    span = pl.program_id(2)
    lane = lax.broadcasted_iota(jnp.int32, (QBLK, LANES), 1)
    head0 = lane < HEAD_DIM
    first = span == 0

    @pl.when(first)
    def _start_of_sequence():
        i = lax.broadcasted_iota(jnp.int32, (WIN, WIN), 0)
        j = lax.broadcasted_iota(jnp.int32, (WIN, WIN), 1)
        slope = jnp.where(i < QBLK, slopes_ref[HEADS_PER_SLAB * slab],
                          slopes_ref[HEADS_PER_SLAB * slab + 1])
        q_pos = i & (QBLK - 1)
        for b, d in enumerate(DILATIONS):
            if d == 1:
                q_at = 4 * (q_pos & (PIECE - 1)) + q_pos // PIECE
                k_at = 4 * (j & (2 * PIECE - 1)) + j // (2 * PIECE)
            else:
                q_at, k_at = q_pos, j
            steps = QBLK + q_at - k_at
            valid = (steps >= 0) & (steps <= QBLK)
            bias = jnp.where(valid, -(slope * LOG2E) * (steps * d).astype(F32), NEG)
            bias_ref[b, 1] = bias
            bias_ref[b, 0] = jnp.where(k_at >= QBLK, bias, NEG)

    not_first = jnp.where(first, 0, 1)
    ones = jnp.ones((WIN, LANES), BF16)

    def before_sequence(prev_blk):
        return jnp.where(first, jnp.zeros_like(prev_blk), prev_blk)

    def window(prev_blk, cur_ref, idx, nb):
        if nb > 0:
            return cur_ref[idx + (slice((nb - 1) * QBLK, (nb + 1) * QBLK), slice(None))]
        return jnp.concatenate([before_sequence(prev_blk),
                                cur_ref[idx + (slice(0, QBLK), slice(None))]], axis=0)

    def window1(prev_ref, cur_ref, nb):
        pieces = []
        for r in range(4):
            if nb > 0:
                pieces.append(cur_ref[0, 0, r, (nb - 1) * PIECE:(nb + 1) * PIECE, :])
            else:
                pieces += [before_sequence(prev_ref[0, 0, r, QBLK - PIECE:QBLK, :]),
                           cur_ref[0, 0, r, 0:PIECE, :]]
        return jnp.concatenate(pieces, axis=0)

    def block(b, q, kw, vw, nb):
        zero = jnp.zeros_like(q)
        qq = jnp.concatenate([jnp.where(head0, q, zero), jnp.where(head0, zero, q)], axis=0)
        s = lax.dot_general(qq, kw, (((1,), (1,)), ((), ())), preferred_element_type=F32)
        s = s + bias_ref[b, not_first if nb == 0 else 1]
        m = jnp.max(s, axis=1, keepdims=True)
        p = jnp.exp2(s - m).astype(BF16)
        acc = jnp.dot(p, jnp.concatenate([vw, ones], axis=1), preferred_element_type=F32)
        o_t = jnp.where(head0, acc[0:QBLK, 0:LANES], acc[QBLK:WIN, 0:LANES])
        l_t = jnp.where(head0, acc[0:QBLK, LANES:], acc[QBLK:WIN, LANES:])
        m_t = jnp.where(head0, m[0:QBLK], m[QBLK:WIN])
        return o_t, l_t, m_t

    def normalised(o_t, l_t, m_t):
        return o_t / l_t, m_t + jnp.log2(l_t)

    for r in range(16):
        o_t, lse_t = normalised(*block(2, qb[0, 0, r], window(kbp[0, 0, r], kb, (0, 0, r), 0),
                                       window(vbp[0, 0, r], vb, (0, 0, r), 0), 0))
        rows = pl.ds((r % 4) * (SPAN // 4) + r // 4, QBLK, stride=4)
        ob[1, rows, :] = o_t
        lb[1, rows, :] = lse_t
    for r in range(4):
        for nb in range(A_BLOCKS):
            q = qa[0, 0, r, nb * QBLK:(nb + 1) * QBLK, :]
            o_t, lse_t = normalised(*block(1, q, window(kap[0, 0, r], ka, (0, 0, r), nb),
                                           window(vap[0, 0, r], va, (0, 0, r), nb), nb))
            rows = slice((r * A_BLOCKS + nb) * QBLK, (r * A_BLOCKS + nb + 1) * QBLK)
            ob[0, rows, :] = o_t
            lb[0, rows, :] = lse_t
    for nb in range(SPAN // QBLK):
        q = jnp.concatenate([qa[0, 0, r, nb * PIECE:(nb + 1) * PIECE, :] for r in range(4)], axis=0)
        o_t, l_t, m_t = block(0, q, window1(kap, ka, nb), window1(vap, va, nb), nb)
        for r in range(4):
            rows = slice(r * (SPAN // 4) + nb * PIECE, r * (SPAN // 4) + (nb + 1) * PIECE)
            piece = slice(r * PIECE, (r + 1) * PIECE)
            m0, l1, l2 = m_t[piece], lb[0, rows, :], lb[1, rows, :]
            mx = jnp.maximum(jnp.maximum(m0, l1), l2)
            w0, w1, w2 = jnp.exp2(m0 - mx), jnp.exp2(l1 - mx), jnp.exp2(l2 - mx)
            num = w0 * o_t[piece] + w1 * ob[0, rows, :] + w2 * ob[1, rows, :]
            tokens = pl.ds(r + 4 * nb * PIECE, PIECE, stride=4)
            o_ref[0, tokens, :] = num / (w0 * l_t[piece] + w1 + w2)


def _attention(qkv_layouts, slopes, batch, seq):
    qa, ka, va, qb, kb, vb = qkv_layouts
    n_slabs = qa.shape[0]
    n = batch * seq
    assert seq % SPAN == 0
    spans = seq // SPAN

    def cur(block_shape):
        zeros = (0,) * (len(block_shape) - 2)
        return pl.BlockSpec(block_shape, lambda b, s, t: (s, b * spans + t) + zeros)

    a_spec = cur((1, 1, 4, SPAN // 4, LANES))
    b_spec = cur((1, 1, 16, QBLK, LANES))
    ap_spec = pl.BlockSpec((1, 1, 4, QBLK, LANES),
                           lambda b, s, t: (s, jnp.maximum(b * spans + t - 1, 0), 0, A_BLOCKS - 1, 0))
    bp_spec = pl.BlockSpec((1, 1, 16, QBLK, LANES),
                           lambda b, s, t: (s, jnp.maximum(b * spans + t - 1, 0), 0, 0, 0))
    return pl.pallas_call(
        _attn_kernel,
        grid=(batch, n_slabs, spans),
        in_specs=[pl.BlockSpec(memory_space=pltpu.SMEM),
                  a_spec, a_spec, a_spec, ap_spec, ap_spec,
                  b_spec, b_spec, b_spec, bp_spec, bp_spec],
        out_specs=pl.BlockSpec((1, SPAN, LANES), lambda b, s, t: (s, b * spans + t, 0)),
        out_shape=jax.ShapeDtypeStruct((n_slabs, n, LANES), F32),
        scratch_shapes=[
            pltpu.VMEM((len(DILATIONS), 2, WIN, WIN), F32),
            pltpu.VMEM((len(DILATIONS) - 1, SPAN, LANES), F32),
            pltpu.VMEM((len(DILATIONS) - 1, SPAN, LANES), F32),
        ],
        compiler_params=pltpu.CompilerParams(
            dimension_semantics=("arbitrary", "arbitrary", "arbitrary"),
            vmem_limit_bytes=VMEM_LIMIT),
        name="attn",
    )(slopes, qa, ka, va, ka, va, qb, kb, vb, kb, vb)


def _tail_kernel(x_ref, attn_ref, sgu_ref, p_ref, ga_ref, wout_ref, gpm_ref, gpf_ref,
                 wgu_ref, wd_ref, gpo_ref, wpg_ref, bpg_ref, wpp_ref, o_ref):
    d_ff = wd_ref.shape[0]
    bounds = (0,) + TAIL_SPLITS + (x_ref.shape[0],)
    chains = [slice(lo, hi) for lo, hi in zip(bounds[:-1], bounds[1:])]

    def dot(a, w_ref):
        return jnp.dot(a, w_ref[...], preferred_element_type=F32)

    pe = [dot(p_ref[rows, :].astype(BF16), wpp_ref) for rows in chains]
    groups = []
    for rows in chains:
        attn = jnp.concatenate([attn_ref[s, rows, :] for s in range(attn_ref.shape[0])], axis=-1)
        an = _rms(attn, ga_ref[...]).astype(BF16)
        groups.append(jnp.concatenate([an, sgu_ref[rows, :]], axis=-1))
    mixed = [dot(g, wout_ref) for g in groups]
    h = [x_ref[rows, :] + _rms(m, gpm_ref[...]) for rows, m in zip(chains, mixed)]
    f = [_rms(hc, gpf_ref[...]).astype(BF16) for hc in h]
    gu = [dot(fc, wgu_ref) for fc in f]
    hid = [(jax.nn.silu(g[:, :d_ff]) * g[:, d_ff:]).astype(BF16) for g in gu]
    y = [dot(hc, wd_ref) for hc in hid]
    h = [hc + _rms(yc, gpo_ref[...]) for hc, yc in zip(h, y)]
    gate = [jax.nn.sigmoid(dot(hc.astype(BF16), wpg_ref) + bpg_ref[...]) for hc in h]
    for rows, hc, gc, pc in zip(chains, h, gate, pe):
        o_ref[rows, :] = hc + gc * pc


def _tail(x2, attn, sgu, p2, *params):
    n, d = x2.shape
    tm = TAIL_TILE

    def tile(a):
        return pl.BlockSpec((tm, a.shape[1]), lambda i: (i, 0))

    return pl.pallas_call(
        _tail_kernel,
        grid=(n // tm,),
        in_specs=[tile(x2), pl.BlockSpec((attn.shape[0], tm, LANES), lambda i: (0, i, 0)),
                  tile(sgu), tile(p2)] + [_const_spec(a.shape) for a in params],
        out_specs=pl.BlockSpec((tm, d), lambda i: (i, 0)),
        out_shape=jax.ShapeDtypeStruct((n, d), F32),
        compiler_params=pltpu.CompilerParams(dimension_semantics=("arbitrary",),
                                             vmem_limit_bytes=VMEM_LIMIT),
        name="tail",
    )(x2, attn, sgu, p2, *params)


def _group_major(w_in, attn_w, sgu_w):
    gd = sgu_w // N_SGU_GROUPS
    u0 = 3 * attn_w
    z0 = u0 + sgu_w
    cols = []
    for g in range(N_SGU_GROUPS):
        cols += [w_in[:, u0 + g * gd:u0 + (g + 1) * gd], w_in[:, z0 + g * gd:z0 + (g + 1) * gd]]
    return jnp.concatenate(cols + [w_in[:, :u0]], axis=1)


def kernel(x, p, ln_pre_mix, w_in, sgu_ln_g, sgu_ln_b, w_spatial, b_spatial, attn_out_norm,
           sgu_out_norm, w_out, ln_post_mix, ln_pre_ffn, w_gate_up, w_down, ln_post_ffn,
           w_pe_gate, b_pe_gate, w_pe_proj):
    batch, seq, d = x.shape
    depth = w_in.shape[0]
    attn_w = attn_out_norm.shape[1]
    sgu_w = sgu_out_norm.shape[1]
    n_heads = attn_w // HEAD_DIM
    slopes = 2.0 ** (-8.0 * (jnp.arange(n_heads, dtype=F32) + 1.0) / n_heads)

    def row(a):
        return a.reshape(1, -1)

    h = x.reshape(batch * seq, d)
    for i in range(depth):
        tail_weights = (w_out[i], w_gate_up[i], w_down[i], w_pe_gate[i], w_pe_proj[i])
        outs = _proj_sgu(
            h, row(ln_pre_mix[i]), _group_major(w_in[i], attn_w, sgu_w).astype(BF16),
            row(sgu_ln_g[i]), row(sgu_ln_b[i]),
            w_spatial[i], b_spatial[i][:, :, None], row(sgu_out_norm[i]), attn_w, sgu_w,
            tail_weights)
        qkv_layouts, sgu = outs[:6], outs[6]
        wout_b, wgu_b, wd_b, wpg_b, wpp_b = outs[7:]
        attn = _attention(qkv_layouts, slopes, batch, seq)
        h = _tail(h, attn, sgu, p[i].reshape(batch * seq, -1),
                  row(attn_out_norm[i]), wout_b, row(ln_post_mix[i]),
                  row(ln_pre_ffn[i]), wgu_b, wd_b,
                  row(ln_post_ffn[i]), wpg_b, row(b_pe_gate[i]), wpp_b)
    return h.reshape(batch, seq, d)
```
